```python
import math
import jax, jax.numpy as jnp
from jax import lax
import numpy as np

D_MODEL = 2048
BATCH = 1
SEQ = 16384
DEPTH = 4

N_MIXERS = 2
N_ATTN_LAYERS = (DEPTH + 1) // 2
N_POOL_LAYERS = DEPTH // 2

N_HEADS = 8
HEAD_DIM = D_MODEL // (2 * N_HEADS)
V_HEAD_DIM = 2 * HEAD_DIM
ROT_DIM = HEAD_DIM // 4
ROPE_THETA = 500000.0
Q_BLOCK = 128
LAMBDA_STD = 0.1

POOL_WINDOWS = (2, 4, 8, 16)
N_POOL_GROUPS = len(POOL_WINDOWS)
POOL_GROUP_DIM = D_MODEL // N_POOL_GROUPS

N_EXPERTS = 64
N_EXPERT_GROUPS = 8
EXPERTS_PER_GROUP = N_EXPERTS // N_EXPERT_GROUPS
TOP_K = 2
D_FF_EXPERT = D_MODEL // 4

DEEPNORM_ALPHA = (2.0 * DEPTH) ** 0.25
DEEPNORM_BETA = (8.0 * DEPTH) ** -0.25
LN_EPS = 1e-5
RMS_EPS = 1e-5

kernel_name = 'hybrid_diffattn_pool_grouped_moe_deepnorm'


def layer_norm(x, g, b):
    xf = x.astype(jnp.float32)
    mu = jnp.mean(xf, axis=-1, keepdims=True)
    var = jnp.mean(jnp.square(xf - mu), axis=-1, keepdims=True)
    y = (xf - mu) * lax.rsqrt(var + LN_EPS) * g.astype(jnp.float32) + b.astype(jnp.float32)
    return y.astype(x.dtype)


def lambda_init_fn(layer):
    return 0.8 - 0.6 * math.exp(-0.3 * layer)


def partial_rotary(t, cos, sin):
    half = ROT_DIM // 2
    tf = t.astype(jnp.float32)
    c = cos[None, :, None, None, :]
    s = sin[None, :, None, None, :]
    t1 = tf[..., :half]
    t2 = tf[..., half:ROT_DIM]
    out = jnp.concatenate([t1 * c - t2 * s, t2 * c + t1 * s, tf[..., ROT_DIM:]], axis=-1)
    return out.astype(t.dtype)


def diff_attention(x, w_qkv, lam_qk, subln_g, w_o, layer):
    B, S, D = x.shape
    qkv = x @ w_qkv
    q, k, v = jnp.split(qkv, 3, axis=-1)
    q = q.reshape(B, S, N_HEADS, 2, HEAD_DIM)
    k = k.reshape(B, S, N_HEADS, 2, HEAD_DIM)
    v = v.reshape(B, S, N_HEADS, V_HEAD_DIM)
    pos = jnp.arange(S, dtype=jnp.float32)
    inv_freq = ROPE_THETA ** (-jnp.arange(0, ROT_DIM, 2, dtype=jnp.float32) / ROT_DIM)
    ang = pos[:, None] * inv_freq[None, :]
    cos, sin = jnp.cos(ang), jnp.sin(ang)
    q = partial_rotary(q, cos, sin) * (HEAD_DIM ** -0.5)
    k = partial_rotary(k, cos, sin)
    lf = lam_qk.astype(jnp.float32)
    lam_init = lambda_init_fn(layer)
    lam = jnp.exp(jnp.sum(lf[0] * lf[1])) - jnp.exp(jnp.sum(lf[2] * lf[3])) + lam_init
    n_blocks = S // Q_BLOCK
    q_blocks = q.reshape(B, n_blocks, Q_BLOCK, N_HEADS, 2, HEAD_DIM).transpose(1, 0, 2, 3, 4, 5)

    def attend(qb):
        s = jnp.einsum('bqhcd,bkhcd->bhcqk', qb, k, preferred_element_type=jnp.float32)
        p = jax.nn.softmax(s, axis=-1)
        a = p[:, :, 0] - lam * p[:, :, 1]
        return jnp.einsum('bhqk,bkhe->bqhe', a.astype(v.dtype), v)

    o = lax.map(attend, q_blocks)
    o = o.transpose(1, 0, 2, 3, 4).reshape(B, S, N_HEADS, V_HEAD_DIM)
    of = o.astype(jnp.float32)
    of = of * lax.rsqrt(jnp.mean(jnp.square(of), axis=-1, keepdims=True) + RMS_EPS)
    of = of * subln_g.astype(jnp.float32) * (1.0 - lam_init)
    return of.astype(x.dtype).reshape(B, S, D) @ w_o


def multiscale_pool(x, w_pool, scale):
    B, S, D = x.shape
    xg = x.astype(jnp.float32).reshape(B, S, N_POOL_GROUPS, POOL_GROUP_DIM)
    csum = jnp.concatenate([jnp.zeros_like(xg[:, :1]), jnp.cumsum(xg, axis=1)], axis=1)
    t = jnp.arange(S)
    diffs = []
    for g, w in enumerate(POOL_WINDOWS):
        half = w // 2
        lo = jnp.clip(t - half, 0, S)
        hi = jnp.clip(t + half, 0, S)
        win_sum = csum[:, hi, g] - csum[:, lo, g]
        cnt = (hi - lo).astype(jnp.float32)[None, :, None]
        diffs.append(win_sum / cnt - xg[:, :, g])
    d = jnp.stack(diffs, axis=2).astype(x.dtype)
    y = jnp.einsum('bsgc,gce->bsge', d, w_pool).reshape(B, S, D)
    return y * scale


def grouped_moe(x, router_w, router_b, w_in, w_out):
    B, S, D = x.shape
    T = B * S
    xt = x.reshape(T, D)
    logits = jnp.dot(xt, router_w, preferred_element_type=jnp.float32)
    scores = jax.nn.sigmoid(logits)
    biased = scores + router_b.astype(jnp.float32)
    grp = biased.reshape(T, N_EXPERT_GROUPS, EXPERTS_PER_GROUP)
    grp_score = jnp.sum(lax.top_k(grp, TOP_K)[0], axis=-1)
    best = jnp.argmax(grp_score, axis=-1)
    in_grp = jnp.arange(N_EXPERT_GROUPS)[None, :] == best[:, None]
    masked = jnp.where(in_grp[:, :, None], grp, -jnp.inf).reshape(T, N_EXPERTS)
    _, idx = lax.top_k(masked, TOP_K)
    gate = jnp.take_along_axis(scores, idx, axis=-1)
    gate = gate / jnp.sum(gate, axis=-1, keepdims=True)
    flat_e = idx.reshape(-1)
    flat_tok = jnp.repeat(jnp.arange(T), TOP_K)
    order = jnp.argsort(flat_e)
    tok_sorted = flat_tok[order]
    gate_sorted = gate.reshape(-1)[order]
    sizes = jnp.bincount(flat_e, length=N_EXPERTS).astype(jnp.int32)
    xs = xt[tok_sorted]
    h = lax.ragged_dot(xs, w_in, sizes)
    hg, hu = jnp.split(h, 2, axis=-1)
    h = jax.nn.silu(hg) * hu
    y = lax.ragged_dot(h, w_out, sizes) * gate_sorted[:, None].astype(x.dtype)
    return jax.ops.segment_sum(y, tok_sorted, num_segments=T).reshape(B, S, D)


def setup_inputs(seed: int = 0) -> dict:
    key = jax.random.key(seed)
    ks = jax.random.split(key, 14)
    f32 = jnp.float32
    D = D_MODEL
    F = D_FF_EXPERT
    C = POOL_GROUP_DIM

    def nrm(k, shape, std):
        return jax.random.normal(k, shape, f32) * std

    x = nrm(ks[0], (BATCH, SEQ, D), 1.0)
    w_qk = nrm(ks[1], (N_ATTN_LAYERS, D, 2 * D), D ** -0.5)
    w_v = nrm(ks[2], (N_ATTN_LAYERS, D, D), DEEPNORM_BETA * D ** -0.5)
    w_qkv = jnp.concatenate([w_qk, w_v], axis=-1)
    lambda_qk = nrm(ks[3], (N_ATTN_LAYERS, 4, HEAD_DIM), LAMBDA_STD)
    subln_g = 1.0 + nrm(ks[4], (N_ATTN_LAYERS, V_HEAD_DIM), 0.02)
    w_o = nrm(ks[5], (N_ATTN_LAYERS, D, D), DEEPNORM_BETA * D ** -0.5)
    pool_w = nrm(ks[6], (N_POOL_LAYERS, N_POOL_GROUPS, C, C), DEEPNORM_BETA * C ** -0.5)
    pool_scale = 1.0 + nrm(ks[7], (N_POOL_LAYERS, D), 0.02)
    router_w = nrm(ks[8], (D, N_EXPERTS), D ** -0.5)
    router_b = nrm(ks[9], (N_EXPERTS,), 0.01)
    moe_w_in = nrm(ks[10], (DEPTH, N_EXPERTS, D, 2 * F), DEEPNORM_BETA * D ** -0.5)
    moe_w_out = nrm(ks[11], (DEPTH, N_EXPERTS, F, D), DEEPNORM_BETA * F ** -0.5)
    ln_g = 1.0 + nrm(ks[12], (DEPTH, 2, D), 0.02)
    ln_b = nrm(ks[13], (DEPTH, 2, D), 0.02)
    return {'x': x, 'w_qkv': w_qkv, 'lambda_qk': lambda_qk, 'subln_g': subln_g, 'w_o': w_o,
            'pool_w': pool_w, 'pool_scale': pool_scale, 'router_w': router_w, 'router_b': router_b,
            'moe_w_in': moe_w_in, 'moe_w_out': moe_w_out, 'ln_g': ln_g, 'ln_b': ln_b}


def reference(x, w_qkv, lambda_qk, subln_g, w_o, pool_w, pool_scale, router_w, router_b,
              moe_w_in, moe_w_out, ln_g, ln_b):
    h = x
    for i in range(DEPTH):
        j = i // N_MIXERS
        if i % N_MIXERS == 0:
            mix = diff_attention(h, w_qkv[j], lambda_qk[j], subln_g[j], w_o[j], i)
        else:
            mix = multiscale_pool(h, pool_w[j], pool_scale[j])
        h = layer_norm(DEEPNORM_ALPHA * h + mix, ln_g[i, 0], ln_b[i, 0])
        ffn = grouped_moe(h, router_w, router_b, moe_w_in[i], moe_w_out[i])
        h = layer_norm(DEEPNORM_ALPHA * h + ffn, ln_g[i, 1], ln_b[i, 1])
    return h
```

```python
import functools
import math

import jax
import jax.numpy as jnp
from jax import lax
from jax.experimental import pallas as pl
from jax.experimental.pallas import tpu as pltpu

F32 = jnp.float32
BF16 = jnp.bfloat16
I32 = jnp.int32

HEAD_DIM = 128
V_HEAD_DIM = 2 * HEAD_DIM
ROT_DIM = HEAD_DIM // 4
ROT_HALF = ROT_DIM // 2
ROPE_THETA = 500000.0
POOL_WINDOWS = (2, 4, 8, 16)
POOL_HALO = 8
N_EXPERT_GROUPS = 8
EXPERTS_PER_GROUP = 8
TOP_K = 2
LN_EPS = 1e-5
RMS_EPS = 1e-5
N_MIXERS = 2
VMEM_LIMIT_BYTES = 56 * 1024 * 1024

TM_QKV, TN_QKV = 1024, 512
TQ_ATTN, TK_ATTN = 1024, 1024
TM_PROJ = 512
TM_POOL = 512
TM_ROUTER = 512
TM_POS = 2048
TM_DISPATCH = 256
TM_EXPERT = 256
TM_COMBINE = 256


def _params(*sem):
    return pltpu.CompilerParams(dimension_semantics=sem, vmem_limit_bytes=VMEM_LIMIT_BYTES)


def _layer_norm(z, g, b):
    mu = jnp.mean(z, axis=-1, keepdims=True)
    zc = z - mu
    var = jnp.mean(zc * zc, axis=-1, keepdims=True)
    return zc * lax.rsqrt(var + LN_EPS) * g + b


def _rope_tables(seq):
    pos = jnp.arange(seq, dtype=F32)
    inv_freq = ROPE_THETA ** (-jnp.arange(0, ROT_DIM, 2, dtype=F32) / ROT_DIM)
    ang = pos[:, None] * inv_freq[None, :]
    cos, sin = jnp.cos(ang), jnp.sin(ang)
    rest = HEAD_DIM - ROT_DIM
    z_half = jnp.zeros((seq, ROT_HALF), F32)
    z_rest = jnp.zeros((seq, rest), F32)
    a = jnp.concatenate([cos, cos, jnp.ones((seq, rest), F32)], axis=1)
    b = jnp.concatenate([z_half, sin, z_rest], axis=1)
    c = jnp.concatenate([-sin, z_half, z_rest], axis=1)
    rot = jnp.stack([a, b, c])
    ident = jnp.stack([jnp.ones((seq, HEAD_DIM), F32), jnp.zeros((seq, HEAD_DIM), F32),
                       jnp.zeros((seq, HEAD_DIM), F32)])
    return jnp.stack([rot * (HEAD_DIM ** -0.5), rot, ident])


def _qkv_kernel(x_ref, w_ref, tab_ref, o_ref, xb_ref):
    @pl.when(pl.program_id(1) == 0)
    def _():
        xb_ref[...] = x_ref[...].astype(BF16)

    y = jnp.dot(xb_ref[...], w_ref[...], preferred_element_type=F32)
    a, b, c = tab_ref[0], tab_ref[1], tab_ref[2]
    for j in range(y.shape[1] // HEAD_DIM):
        sl = slice(j * HEAD_DIM, (j + 1) * HEAD_DIM)
        yj = y[:, sl]
        out = yj * a + pltpu.roll(yj, ROT_HALF, 1) * b + pltpu.roll(yj, HEAD_DIM - ROT_HALF, 1) * c
        o_ref[:, sl] = out.astype(o_ref.dtype)


def _qkv_proj(h, w_bf16, tables):
    t, d = h.shape
    n = w_bf16.shape[1]
    tm, tn = min(TM_QKV, t), min(TN_QKV, d)
    return pl.pallas_call(
        _qkv_kernel,
        out_shape=jax.ShapeDtypeStruct((t, n), BF16),
        grid=(t // tm, n // tn),
        in_specs=[
            pl.BlockSpec((tm, d), lambda m, j: (m, 0)),
            pl.BlockSpec((d, tn), lambda m, j: (0, j)),
            pl.BlockSpec((None, 3, tm, HEAD_DIM), lambda m, j: ((j * tn) // d, 0, m, 0)),
        ],
        out_specs=pl.BlockSpec((tm, tn), lambda m, j: (m, j)),
        scratch_shapes=[pltpu.VMEM((tm, d), BF16)],
        compiler_params=_params("parallel", "arbitrary"),
        name="qkv_rope",
    )(h, w_bf16, tables)


def _attn_kernel(lam_ref, g_ref, q_ref, k_ref, v_ref, o_ref, m_ref, l_ref, acc_ref, *, tk, lam_init):
    seq = k_ref.shape[0]
    m_ref[...] = jnp.full(m_ref.shape, -jnp.inf, F32)
    l_ref[...] = jnp.zeros(l_ref.shape, F32)
    acc_ref[...] = jnp.zeros(acc_ref.shape, F32)

    def body(j, carry):
        start = pl.multiple_of(j * tk, tk)
        ks = k_ref[pl.ds(start, tk), :]
        vs = v_ref[pl.ds(start, tk), :]
        for c in range(2):
            qc = q_ref[:, c * HEAD_DIM:(c + 1) * HEAD_DIM]
            kc = ks[:, c * HEAD_DIM:(c + 1) * HEAD_DIM]
            s = lax.dot_general(qc, kc, (((1,), (1,)), ((), ())), preferred_element_type=F32)
            m_prev = m_ref[c]
            m_new = jnp.maximum(m_prev, jnp.max(s, axis=1, keepdims=True))
            alpha = jnp.exp(m_prev - m_new)
            p = jnp.exp(s - m_new)
            l_ref[c] = alpha * l_ref[c] + jnp.sum(p, axis=1, keepdims=True)
            acc_ref[c] = alpha * acc_ref[c] + jnp.dot(p.astype(BF16), vs, preferred_element_type=F32)
            m_ref[c] = m_new
        return carry

    lax.fori_loop(0, seq // tk, body, 0)

    lq = lam_ref[...]
    lam = (jnp.exp(jnp.sum(lq[0:1] * lq[1:2], axis=1, keepdims=True))
           - jnp.exp(jnp.sum(lq[2:3] * lq[3:4], axis=1, keepdims=True)) + lam_init)
    o = acc_ref[0] / l_ref[0] - lam * (acc_ref[1] / l_ref[1])
    ms = jnp.mean(o * o, axis=1, keepdims=True)
    of = o * lax.rsqrt(ms + RMS_EPS)
    of = of * g_ref[...] * (1.0 - lam_init)
    o_ref[...] = of.astype(o_ref.dtype)


def _diff_attention(qkv, lam_qk, subln_g, lam_init):
    t = qkv.shape[0]
    d = qkv.shape[1] // 3
    n_heads = d // V_HEAD_DIM
    tq, tk = min(TQ_ATTN, t), min(TK_ATTN, t)
    kern = functools.partial(_attn_kernel, tk=tk, lam_init=lam_init)
    return pl.pallas_call(
        kern,
        out_shape=jax.ShapeDtypeStruct((t, d), BF16),
        grid=(n_heads, t // tq),
        in_specs=[
            pl.BlockSpec((4, HEAD_DIM), lambda h, i: (0, 0)),
            pl.BlockSpec((1, V_HEAD_DIM), lambda h, i: (0, 0)),
            pl.BlockSpec((tq, V_HEAD_DIM), lambda h, i: (i, h)),
            pl.BlockSpec((t, V_HEAD_DIM), lambda h, i: (0, n_heads + h), pipeline_mode=pl.Buffered(1)),
            pl.BlockSpec((t, V_HEAD_DIM), lambda h, i: (0, 2 * n_heads + h), pipeline_mode=pl.Buffered(1)),
        ],
        out_specs=pl.BlockSpec((tq, V_HEAD_DIM), lambda h, i: (i, h)),
        scratch_shapes=[
            pltpu.VMEM((2, tq, 1), F32),
            pltpu.VMEM((2, tq, 1), F32),
            pltpu.VMEM((2, tq, V_HEAD_DIM), F32),
        ],
        compiler_params=_params("parallel", "parallel"),
        name="diff_attn",
    )(lam_qk, subln_g.reshape(1, V_HEAD_DIM), qkv, qkv, qkv)


def _proj_ln_kernel(a_ref, w_ref, h_ref, g_ref, b_ref, o_ref, *, alpha):
    mix = jnp.dot(a_ref[...], w_ref[...], preferred_element_type=F32)
    z = alpha * h_ref[...] + mix
    o_ref[...] = _layer_norm(z, g_ref[...], b_ref[...])


def _proj_ln(a, w_bf16, h, g, b, alpha):
    t, d = h.shape
    tm = min(TM_PROJ, t)
    return pl.pallas_call(
        functools.partial(_proj_ln_kernel, alpha=alpha),
        out_shape=jax.ShapeDtypeStruct((t, d), F32),
        grid=(t // tm,),
        in_specs=[
            pl.BlockSpec((tm, d), lambda i: (i, 0)),
            pl.BlockSpec((d, d), lambda i: (0, 0), pipeline_mode=pl.Buffered(1)),
            pl.BlockSpec((tm, d), lambda i: (i, 0)),
            pl.BlockSpec((1, d), lambda i: (0, 0)),
            pl.BlockSpec((1, d), lambda i: (0, 0)),
        ],
        out_specs=pl.BlockSpec((tm, d), lambda i: (i, 0)),
        compiler_params=_params("parallel"),
        name="proj_ln",
    )(a, w_bf16, h, g.reshape(1, d), b.reshape(1, d))


def _pool_ln_kernel(h_ref, hp_ref, hn_ref, w_ref, sc_ref, g_ref, b_ref, o_ref, ext_ref, z_ref, *, alpha, seq):
    i = pl.program_id(0)
    tm, d = h_ref.shape
    c = d // len(POOL_WINDOWS)
    ext_ref[0:POOL_HALO, :] = jnp.where(i > 0, hp_ref[...], 0.0)
    ext_ref[POOL_HALO:POOL_HALO + tm, :] = h_ref[...]
    ext_ref[POOL_HALO + tm:2 * POOL_HALO + tm, :] = jnp.where(i < pl.num_programs(0) - 1, hn_ref[...], 0.0)
    t = i * tm + lax.broadcasted_iota(I32, (tm, 1), 0)
    for g, w in enumerate(POOL_WINDOWS):
        half = w // 2
        cols = slice(g * c, (g + 1) * c)
        win = ext_ref[POOL_HALO - half:POOL_HALO - half + tm, cols]
        for j in range(1 - half, half):
            win = win + ext_ref[POOL_HALO + j:POOL_HALO + j + tm, cols]
        cnt = (jnp.minimum(t + half, seq) - jnp.maximum(t - half, 0)).astype(F32)
        x = h_ref[:, cols]
        diff = win / cnt - x
        y = jnp.dot(diff.astype(BF16), w_ref[g], preferred_element_type=F32) * sc_ref[:, cols]
        z_ref[:, cols] = alpha * x + y
    o_ref[...] = _layer_norm(z_ref[...], g_ref[...], b_ref[...])


def _pool_ln(h, w_bf16, scale, g, b, alpha):
    t, d = h.shape
    tm = min(TM_POOL, t)
    hb = tm // POOL_HALO
    n_halo = t // POOL_HALO
    c = d // len(POOL_WINDOWS)
    return pl.pallas_call(
        functools.partial(_pool_ln_kernel, alpha=alpha, seq=t),
        out_shape=jax.ShapeDtypeStruct((t, d), F32),
        grid=(t // tm,),
        in_specs=[
            pl.BlockSpec((tm, d), lambda i: (i, 0)),
            pl.BlockSpec((POOL_HALO, d), lambda i: (jnp.maximum(i * hb - 1, 0), 0)),
            pl.BlockSpec((POOL_HALO, d), lambda i: (jnp.minimum((i + 1) * hb, n_halo - 1), 0)),
            pl.BlockSpec((len(POOL_WINDOWS), c, c), lambda i: (0, 0, 0)),
            pl.BlockSpec((1, d), lambda i: (0, 0)),
            pl.BlockSpec((1, d), lambda i: (0, 0)),
            pl.BlockSpec((1, d), lambda i: (0, 0)),
        ],
        out_specs=pl.BlockSpec((tm, d), lambda i: (i, 0)),
        scratch_shapes=[pltpu.VMEM((tm + 2 * POOL_HALO, d), F32), pltpu.VMEM((tm, d), F32)],
        compiler_params=_params("parallel"),
        name="pool_ln",
    )(h, h, h, w_bf16, scale.reshape(1, d), g.reshape(1, d), b.reshape(1, d))


def _split_bf16(x):
    hi = x.astype(BF16)
    lo = (x - hi.astype(F32)).astype(BF16)
    return hi, lo


def _router_kernel(h_ref, wt_ref, b_ref, idx_ref, gate_ref, rank_ref, cnt_ref, carry_ref, tri_ref):
    tm = h_ref.shape[0]
    n_exp = wt_ref.shape[0]
    n_grp, per = N_EXPERT_GROUPS, EXPERTS_PER_GROUP

    @pl.when(pl.program_id(0) == 0)
    def _():
        carry_ref[...] = jnp.zeros(carry_ref.shape, F32)
        r = lax.broadcasted_iota(I32, (tm, tm), 0)
        c = lax.broadcasted_iota(I32, (tm, tm), 1)
        tri_ref[...] = jnp.where(r <= c, 1.0, 0.0).astype(BF16)

    h_hi, h_lo = _split_bf16(h_ref[...])
    w_hi, w_lo = _split_bf16(wt_ref[...])
    dn = (((1,), (1,)), ((), ()))
    logits = (lax.dot_general(w_hi, h_hi, dn, preferred_element_type=F32)
              + lax.dot_general(w_hi, h_lo, dn, preferred_element_type=F32)
              + lax.dot_general(w_lo, h_hi, dn, preferred_element_type=F32))
    scores = 1.0 / (1.0 + jnp.exp(-logits))
    biased = scores + b_ref[...]

    v = biased.reshape(n_grp, per, tm)
    piota = lax.broadcasted_iota(I32, (n_grp, per, tm), 1)
    m1 = jnp.max(v, axis=1, keepdims=True)
    i1 = jnp.min(jnp.where(v == m1, piota, per), axis=1, keepdims=True)
    v2 = jnp.where(piota == i1, -jnp.inf, v)
    m2 = jnp.max(v2, axis=1, keepdims=True)
    i2 = jnp.min(jnp.where(v2 == m2, piota, per), axis=1, keepdims=True)
    grp_score = m1 + m2
    giota = lax.broadcasted_iota(I32, (n_grp, 1, tm), 0)
    best = jnp.min(jnp.where(grp_score == jnp.max(grp_score, axis=0, keepdims=True), giota, n_grp),
                   axis=0, keepdims=True)
    sel = giota == best
    e0 = jnp.sum(jnp.where(sel, giota * per + i1, 0), axis=0)
    e1 = jnp.sum(jnp.where(sel, giota * per + i2, 0), axis=0)

    eiota = lax.broadcasted_iota(I32, (n_exp, tm), 0)
    oh0 = eiota == e0
    oh1 = eiota == e1
    s0 = jnp.sum(jnp.where(oh0, scores, 0.0), axis=0, keepdims=True)
    s1 = jnp.sum(jnp.where(oh1, scores, 0.0), axis=0, keepdims=True)
    den = s0 + s1

    member = jnp.where(oh0, 1.0, jnp.where(oh1, 1.0, 0.0)).astype(BF16)
    cnt = jnp.dot(member, tri_ref[...], preferred_element_type=F32)
    tot = cnt + carry_ref[:, 0:1]
    r0 = jnp.sum(jnp.where(oh0, tot, 0.0), axis=0, keepdims=True) - 1.0
    r1 = jnp.sum(jnp.where(oh1, tot, 0.0), axis=0, keepdims=True) - 1.0
    carry_ref[...] = carry_ref[...] + cnt[:, tm - 1:tm]

    idx_ref[0:1, :] = e0
    idx_ref[1:2, :] = e1
    gate_ref[0:1, :] = s0 / den
    gate_ref[1:2, :] = s1 / den
    rank_ref[0:1, :] = r0.astype(I32)
    rank_ref[1:2, :] = r1.astype(I32)
    cnt_ref[...] = carry_ref[...]


def _router(h, router_wt, router_b):
    t, d = h.shape
    n_exp = router_wt.shape[0]
    tm = min(TM_ROUTER, t)
    pair = lambda dt: jax.ShapeDtypeStruct((TOP_K, t), dt)
    pair_spec = pl.BlockSpec((TOP_K, tm), lambda i: (0, i))
    return pl.pallas_call(
        _router_kernel,
        out_shape=(pair(I32), pair(F32), pair(I32), jax.ShapeDtypeStruct((n_exp, HEAD_DIM), F32)),
        grid=(t // tm,),
        in_specs=[
            pl.BlockSpec((tm, d), lambda i: (i, 0)),
            pl.BlockSpec((n_exp, d), lambda i: (0, 0)),
            pl.BlockSpec((n_exp, 1), lambda i: (0, 0)),
        ],
        out_specs=(pair_spec, pair_spec, pair_spec, pl.BlockSpec((n_exp, HEAD_DIM), lambda i: (0, 0))),
        scratch_shapes=[pltpu.VMEM((n_exp, HEAD_DIM), F32), pltpu.VMEM((tm, tm), BF16)],
        compiler_params=_params("arbitrary"),
        name="router",
    )(h, router_wt, router_b.reshape(n_exp, 1))


def _pos_kernel(idx_ref, rank_ref, starts_ref, pos_ref):
    n_exp = starts_ref.shape[0]
    tm = idx_ref.shape[1]
    eiota = lax.broadcasted_iota(I32, (n_exp, tm), 0)
    st = starts_ref[...]
    for k in range(TOP_K):
        oh = eiota == idx_ref[k:k + 1, :]
        pos_ref[k:k + 1, :] = jnp.sum(jnp.where(oh, st, 0), axis=0, keepdims=True) + rank_ref[k:k + 1, :]


def _positions(idx, rank, starts):
    t = idx.shape[1]
    n_exp = starts.shape[0]
    tm = min(TM_POS, t)
    pair_spec = pl.BlockSpec((TOP_K, tm), lambda i: (0, i))
    return pl.pallas_call(
        _pos_kernel,
        out_shape=jax.ShapeDtypeStruct((TOP_K, t), I32),
        grid=(t // tm,),
        in_specs=[pair_spec, pair_spec, pl.BlockSpec((n_exp, 1), lambda i: (0, 0))],
        out_specs=pair_spec,
        compiler_params=_params("parallel"),
        name="positions",
    )(idx, rank, starts.reshape(n_exp, 1))


def _dispatch_kernel(pos_ref, h_ref, xs_ref, sem):
    tm = h_ref.shape[0]
    n_tok = pos_ref.shape[0] // TOP_K
    base = pl.program_id(0) * tm

    def row_copy(r, k):
        p = pos_ref[k * n_tok + base + r]
        return pltpu.make_async_copy(h_ref.at[pl.ds(r, 1), :], xs_ref.at[pl.ds(p, 1), :], sem.at[k])

    def issue(r, carry):
        for k in range(TOP_K):
            row_copy(r, k).start()
        return carry

    def drain(r, carry):
        for k in range(TOP_K):
            row_copy(r, k).wait()
        return carry

    lax.fori_loop(0, tm, issue, 0)
    lax.fori_loop(0, tm, drain, 0)


def _dispatch(h, pos_flat):
    t, d = h.shape
    tm = min(TM_DISPATCH, t)
    return pl.pallas_call(
        _dispatch_kernel,
        out_shape=jax.ShapeDtypeStruct((TOP_K * t, d), F32),
        grid_spec=pltpu.PrefetchScalarGridSpec(
            num_scalar_prefetch=1,
            grid=(t // tm,),
            in_specs=[pl.BlockSpec((tm, d), lambda i, pos: (i, 0))],
            out_specs=pl.BlockSpec(memory_space=pl.ANY),
            scratch_shapes=[pltpu.SemaphoreType.DMA((TOP_K,))],
        ),
        compiler_params=_params("arbitrary"),
        name="dispatch",
    )(pos_flat, h)


def _expert_kernel(tile_ref, exp_ref, lo_ref, hi_ref, nw_ref, x_ref, win_ref, wout_ref, o_ref, winb_ref, woutb_ref):
    w = pl.program_id(0)
    tm = x_ref.shape[0]
    d_ff = wout_ref.shape[0]
    prev = jnp.maximum(w - 1, 0)
    new_expert = jnp.logical_or(w == 0, exp_ref[w] != exp_ref[prev])
    new_tile = jnp.logical_or(w == 0, tile_ref[w] != tile_ref[prev])
    active = w < nw_ref[0]

    @pl.when(jnp.logical_and(active, new_expert))
    def _():
        winb_ref[...] = win_ref[...].astype(BF16)
        woutb_ref[...] = wout_ref[...].astype(BF16)

    @pl.when(active)
    def _():
        hid = jnp.dot(x_ref[...].astype(BF16), winb_ref[...], preferred_element_type=F32)
        hg, hu = hid[:, :d_ff], hid[:, d_ff:]
        act = hg * (1.0 / (1.0 + jnp.exp(-hg))) * hu
        y = jnp.dot(act.astype(BF16), woutb_ref[...], preferred_element_type=F32)
        rows = lax.broadcasted_iota(I32, (tm, 1), 0)
        mine = jnp.logical_and(rows >= lo_ref[w], rows < hi_ref[w])

        @pl.when(new_tile)
        def _():
            o_ref[...] = jnp.where(mine, y, 0.0)

        @pl.when(jnp.logical_not(new_tile))
        def _():
            o_ref[...] = jnp.where(mine, y, o_ref[...])


def _work_items(counts, tm, n_rows):
    n_exp = counts.shape[0]
    n_tiles = n_rows // tm
    n_work = n_tiles + n_exp - 1
    ends = jnp.cumsum(counts)
    starts = ends - counts
    first = starts // tm
    last = jnp.maximum(ends - 1, starts) // tm
    n_items = jnp.where(counts > 0, last - first + 1, 0)
    item_end = jnp.cumsum(n_items)
    item_start = item_end - n_items
    total = item_end[-1]
    w = jnp.arange(n_work, dtype=I32)
    wc = jnp.minimum(w, total - 1)
    e = jnp.searchsorted(item_end, wc, side="right").astype(I32)
    tile = first[e] + (wc - item_start[e])
    lo = jnp.maximum(starts[e], tile * tm) - tile * tm
    hi = jnp.minimum(ends[e], (tile + 1) * tm) - tile * tm
    return tile.astype(I32), e, lo.astype(I32), hi.astype(I32), total.reshape(1).astype(I32), starts.astype(I32)


def _experts(xs, w_in, w_out, items):
    p, d = xs.shape
    n_exp, _, two_f = w_in.shape
    d_ff = w_out.shape[1]
    tm = min(TM_EXPERT, p)
    tile, e, lo, hi, total = items
    n_work = tile.shape[0]
    return pl.pallas_call(
        _expert_kernel,
        out_shape=jax.ShapeDtypeStruct((p, d), F32),
        grid_spec=pltpu.PrefetchScalarGridSpec(
            num_scalar_prefetch=5,
            grid=(n_work,),
            in_specs=[
                pl.BlockSpec((tm, d), lambda w, t, e, lo, hi, n: (t[w], 0)),
                pl.BlockSpec((None, d, two_f), lambda w, t, e, lo, hi, n: (e[w], 0, 0)),
                pl.BlockSpec((None, d_ff, d), lambda w, t, e, lo, hi, n: (e[w], 0, 0)),
            ],
            out_specs=pl.BlockSpec((tm, d), lambda w, t, e, lo, hi, n: (t[w], 0)),
            scratch_shapes=[pltpu.VMEM((d, two_f), BF16), pltpu.VMEM((d_ff, d), BF16)],
        ),
        compiler_params=_params("arbitrary"),
        name="experts",
    )(tile, e, lo, hi, total, xs, w_in, w_out)


def _combine_kernel(pos_ref, h_ref, gate_ref, g_ref, b_ref, y_ref, o_ref, ybuf_ref, sem, *, alpha):
    tm = h_ref.shape[0]
    n_tok = pos_ref.shape[0] // TOP_K
    base = pl.program_id(0) * tm

    def row_copy(r, k):
        p = pos_ref[k * n_tok + base + r]
        return pltpu.make_async_copy(y_ref.at[pl.ds(p, 1), :], ybuf_ref.at[k, pl.ds(r, 1), :], sem.at[k])

    def issue(r, carry):
        for k in range(TOP_K):
            row_copy(r, k).start()
        return carry

    def drain(r, carry):
        for k in range(TOP_K):
            row_copy(r, k).wait()
        return carry

    lax.fori_loop(0, tm, issue, 0)
    lax.fori_loop(0, tm, drain, 0)
    gate = gate_ref[...]
    ffn = ybuf_ref[0] * gate[:, 0:1] + ybuf_ref[1] * gate[:, 1:2]
    z = alpha * h_ref[...] + ffn
    o_ref[...] = _layer_norm(z, g_ref[...], b_ref[...])


def _combine_ln(h, y_sorted, pos_flat, gate_tk, g, b, alpha):
    t, d = h.shape
    tm = min(TM_COMBINE, t)
    return pl.pallas_call(
        functools.partial(_combine_kernel, alpha=alpha),
        out_shape=jax.ShapeDtypeStruct((t, d), F32),
        grid_spec=pltpu.PrefetchScalarGridSpec(
            num_scalar_prefetch=1,
            grid=(t // tm,),
            in_specs=[
                pl.BlockSpec((tm, d), lambda i, pos: (i, 0)),
                pl.BlockSpec((tm, TOP_K), lambda i, pos: (i, 0)),
                pl.BlockSpec((1, d), lambda i, pos: (0, 0)),
                pl.BlockSpec((1, d), lambda i, pos: (0, 0)),
                pl.BlockSpec(memory_space=pl.ANY),
            ],
            out_specs=pl.BlockSpec((tm, d), lambda i, pos: (i, 0)),
            scratch_shapes=[pltpu.VMEM((TOP_K, tm, d), F32), pltpu.SemaphoreType.DMA((TOP_K,))],
        ),
        compiler_params=_params("arbitrary"),
        name="combine_ln",
    )(pos_flat, h, gate_tk, g.reshape(1, d), b.reshape(1, d), y_sorted)


def _moe_ln(h, router_wt, router_b, w_in, w_out, g, b, alpha):
    t = h.shape[0]
    idx, gate, rank, cnt = _router(h, router_wt, router_b)
    counts = cnt[:, 0].astype(I32)
    tile, e, lo, hi, total, starts = _work_items(counts, min(TM_EXPERT, TOP_K * t), TOP_K * t)
    pos_flat = _positions(idx, rank, starts).reshape(TOP_K * t)
    xs = _dispatch(h, pos_flat)
    ys = _experts(xs, w_in, w_out, (tile, e, lo, hi, total))
    return _combine_ln(h, ys, pos_flat, gate.T, g, b, alpha)


def kernel(x, w_qkv, lambda_qk, subln_g, w_o, pool_w, pool_scale, router_w, router_b, moe_w_in, moe_w_out,
           ln_g, ln_b):
    batch, seq, d = x.shape
    depth = ln_g.shape[0]
    alpha = (2.0 * depth) ** 0.25
    tables = _rope_tables(seq)
    router_wt = router_w.T
    outs = []
    for bi in range(batch):
        h = x[bi]
        for i in range(depth):
            j = i // N_MIXERS
            if i % N_MIXERS == 0:
                lam_init = 0.8 - 0.6 * math.exp(-0.3 * i)
                qkv = _qkv_proj(h, w_qkv[j].astype(BF16), tables)
                att = _diff_attention(qkv, lambda_qk[j], subln_g[j], lam_init)
                h = _proj_ln(att, w_o[j].astype(BF16), h, ln_g[i, 0], ln_b[i, 0], alpha)
            else:
                h = _pool_ln(h, pool_w[j].astype(BF16), pool_scale[j], ln_g[i, 0], ln_b[i, 0], alpha)
            h = _moe_ln(h, router_wt, router_b, moe_w_in[i], moe_w_out[i], ln_g[i, 1], ln_b[i, 1], alpha)
        outs.append(h)
    return jnp.stack(outs)
```

```python
import functools
import math

import jax
import jax.numpy as jnp
from jax import lax
from jax.experimental import pallas as pl
from jax.experimental.pallas import tpu as pltpu

F32 = jnp.float32
BF16 = jnp.bfloat16
I32 = jnp.int32

HEAD_DIM = 128
V_HEAD_DIM = 2 * HEAD_DIM
ROT_DIM = HEAD_DIM // 4
ROT_HALF = ROT_DIM // 2
ROPE_THETA = 500000.0
POOL_WINDOWS = (2, 4, 8, 16)
POOL_HALO = 8
N_EXPERT_GROUPS = 8
EXPERTS_PER_GROUP = 8
TOP_K = 2
LN_EPS = 1e-5
RMS_EPS = 1e-5
N_MIXERS = 2
VMEM_LIMIT_BYTES = 56 * 1024 * 1024

TM_QKV, TN_QKV = 1024, 512
TQ_ATTN, TK_ATTN = 512, 1024
TM_PROJ = 512
TM_POOL = 512
TM_ROUTER = 512
TM_POS = 2048
TM_DISPATCH = 256
TM_EXPERT = 256
TM_COMBINE = 256


def _params(*sem):
    return pltpu.CompilerParams(dimension_semantics=sem, vmem_limit_bytes=VMEM_LIMIT_BYTES)


def _layer_norm(z, g, b):
    mu = jnp.mean(z, axis=-1, keepdims=True)
    zc = z - mu
    var = jnp.mean(zc * zc, axis=-1, keepdims=True)
    return zc * lax.rsqrt(var + LN_EPS) * g + b


def _rope_tables(seq):
    pos = jnp.arange(seq, dtype=F32)
    inv_freq = ROPE_THETA ** (-jnp.arange(0, ROT_DIM, 2, dtype=F32) / ROT_DIM)
    ang = pos[:, None] * inv_freq[None, :]
    cos, sin = jnp.cos(ang), jnp.sin(ang)
    rest = HEAD_DIM - ROT_DIM
    z_half = jnp.zeros((seq, ROT_HALF), F32)
    z_rest = jnp.zeros((seq, rest), F32)
    a = jnp.concatenate([cos, cos, jnp.ones((seq, rest), F32)], axis=1)
    b = jnp.concatenate([z_half, sin, z_rest], axis=1)
    c = jnp.concatenate([-sin, z_half, z_rest], axis=1)
    rot = jnp.stack([a, b, c])
    ident = jnp.stack([jnp.ones((seq, HEAD_DIM), F32), jnp.zeros((seq, HEAD_DIM), F32),
                       jnp.zeros((seq, HEAD_DIM), F32)])
    return jnp.stack([rot * (HEAD_DIM ** -0.5 * math.log2(math.e)), rot, ident])


def _qkv_kernel(x_ref, w_ref, tab_ref, o_ref, xb_ref):
    @pl.when(pl.program_id(1) == 0)
    def _():
        xb_ref[...] = x_ref[...].astype(BF16)

    y = jnp.dot(xb_ref[...], w_ref[...], preferred_element_type=F32)
    a, b, c = tab_ref[0], tab_ref[1], tab_ref[2]
    for j in range(y.shape[1] // HEAD_DIM):
        sl = slice(j * HEAD_DIM, (j + 1) * HEAD_DIM)
        yj = y[:, sl]
        out = yj * a + pltpu.roll(yj, ROT_HALF, 1) * b + pltpu.roll(yj, HEAD_DIM - ROT_HALF, 1) * c
        o_ref[:, sl] = out.astype(o_ref.dtype)


def _qkv_proj(h, w_bf16, tables):
    t, d = h.shape
    n = w_bf16.shape[1]
    tm, tn = min(TM_QKV, t), min(TN_QKV, d)
    return pl.pallas_call(
        _qkv_kernel,
        out_shape=jax.ShapeDtypeStruct((t, n), BF16),
        grid=(t // tm, n // tn),
        in_specs=[
            pl.BlockSpec((tm, d), lambda m, j: (m, 0)),
            pl.BlockSpec((d, tn), lambda m, j: (0, j)),
            pl.BlockSpec((None, 3, tm, HEAD_DIM), lambda m, j: ((j * tn) // d, 0, m, 0)),
        ],
        out_specs=pl.BlockSpec((tm, tn), lambda m, j: (m, j)),
        scratch_shapes=[pltpu.VMEM((tm, d), BF16)],
        compiler_params=_params("parallel", "arbitrary"),
        name="qkv_rope",
    )(h, w_bf16, tables)


def _attn_kernel(lam_ref, g_ref, q_ref, k_ref, v_ref, o_ref,
                 s0_ref, s1_ref, p0_ref, p1_ref, mx0_ref, mx1_ref, al0_ref, al1_ref,
                 m_ref, l_ref, acc_ref, *, tk, lam_init):
    n = k_ref.shape[0] // tk
    s_refs, p_refs = (s0_ref, s1_ref), (p0_ref, p1_ref)
    mx_refs, al_refs = (mx0_ref, mx1_ref), (al0_ref, al1_ref)
    m_ref[...] = jnp.full(m_ref.shape, -jnp.inf, F32)
    l_ref[...] = jnp.zeros(l_ref.shape, F32)
    acc_ref[...] = jnp.zeros(acc_ref.shape, F32)

    def scores(j, slot):
        start = pl.multiple_of(j * tk, tk)
        for c in range(2):
            cols = slice(c * HEAD_DIM, (c + 1) * HEAD_DIM)
            s = lax.dot_general(q_ref[:, cols], k_ref[pl.ds(start, tk), cols],
                                (((1,), (1,)), ((), ())), preferred_element_type=F32)
            s_refs[slot][c] = s
            mx_refs[slot][c] = jnp.max(s, axis=1, keepdims=True)

    def probs(slot):
        for c in range(2):
            m_prev = m_ref[c]
            m_new = jnp.maximum(m_prev, mx_refs[slot][c])
            alpha = jnp.exp2(m_prev - m_new)
            p = jnp.exp2(s_refs[slot][c] - m_new)
            l_ref[c] = alpha * l_ref[c] + jnp.sum(p, axis=1, keepdims=True)
            p_refs[slot][c] = p.astype(BF16)
            al_refs[slot][c] = alpha
            m_ref[c] = m_new

    def accumulate(j, slot):
        start = pl.multiple_of(j * tk, tk)
        vs = v_ref[pl.ds(start, tk), :]
        for c in range(2):
            acc_ref[c] = al_refs[slot][c] * acc_ref[c] + jnp.dot(p_refs[slot][c], vs, preferred_element_type=F32)

    scores(0, 0)
    if n > 1:
        probs(0)
        scores(1, 1)

        def body(i, carry):
            j = 2 * i + 1
            accumulate(j - 1, 0)
            probs(1)
            scores(j + 1, 0)
            accumulate(j, 1)
            probs(0)
            scores(j + 2, 1)
            return carry

        lax.fori_loop(0, (n - 2) // 2, body, 0)
        accumulate(n - 2, 0)
    probs((n - 1) % 2)
    accumulate(n - 1, (n - 1) % 2)

    lq = lam_ref[...]
    lam = (jnp.exp(jnp.sum(lq[0:1] * lq[1:2], axis=1, keepdims=True))
           - jnp.exp(jnp.sum(lq[2:3] * lq[3:4], axis=1, keepdims=True)) + lam_init)
    o = acc_ref[0] / l_ref[0] - lam * (acc_ref[1] / l_ref[1])
    ms = jnp.mean(o * o, axis=1, keepdims=True)
    of = o * lax.rsqrt(ms + RMS_EPS)
    of = of * g_ref[...] * (1.0 - lam_init)
    o_ref[...] = of.astype(o_ref.dtype)


def _diff_attention(qkv, lam_qk, subln_g, lam_init):
    t = qkv.shape[0]
    d = qkv.shape[1] // 3
    n_heads = d // V_HEAD_DIM
    tq, tk = min(TQ_ATTN, t), min(TK_ATTN, t)
    assert t // tk == 1 or (t // tk) % 2 == 0
    kern = functools.partial(_attn_kernel, tk=tk, lam_init=lam_init)
    slab = lambda dt: pltpu.VMEM((2, tq, tk), dt)
    stat = pltpu.VMEM((2, tq, 1), F32)
    return pl.pallas_call(
        kern,
        out_shape=jax.ShapeDtypeStruct((t, d), BF16),
        grid=(n_heads, t // tq),
        in_specs=[
            pl.BlockSpec((4, HEAD_DIM), lambda h, i: (0, 0)),
            pl.BlockSpec((1, V_HEAD_DIM), lambda h, i: (0, 0)),
            pl.BlockSpec((tq, V_HEAD_DIM), lambda h, i: (i, h)),
            pl.BlockSpec((t, V_HEAD_DIM), lambda h, i: (0, n_heads + h), pipeline_mode=pl.Buffered(1)),
            pl.BlockSpec((t, V_HEAD_DIM), lambda h, i: (0, 2 * n_heads + h), pipeline_mode=pl.Buffered(1)),
        ],
        out_specs=pl.BlockSpec((tq, V_HEAD_DIM), lambda h, i: (i, h)),
        scratch_shapes=[
            slab(F32), slab(F32), slab(BF16), slab(BF16),
            stat, stat, stat, stat,
            stat, stat,
            pltpu.VMEM((2, tq, V_HEAD_DIM), F32),
        ],
        compiler_params=_params("parallel", "parallel"),
        name="diff_attn",
    )(lam_qk, subln_g.reshape(1, V_HEAD_DIM), qkv, qkv, qkv)


def _proj_ln_kernel(a_ref, w_ref, h_ref, g_ref, b_ref, o_ref, *, alpha):
    mix = jnp.dot(a_ref[...], w_ref[...], preferred_element_type=F32)
    z = alpha * h_ref[...] + mix
    o_ref[...] = _layer_norm(z, g_ref[...], b_ref[...])


def _proj_ln(a, w_bf16, h, g, b, alpha):
    t, d = h.shape
    tm = min(TM_PROJ, t)
    return pl.pallas_call(
        functools.partial(_proj_ln_kernel, alpha=alpha),
        out_shape=jax.ShapeDtypeStruct((t, d), F32),
        grid=(t // tm,),
        in_specs=[
            pl.BlockSpec((tm, d), lambda i: (i, 0)),
            pl.BlockSpec((d, d), lambda i: (0, 0), pipeline_mode=pl.Buffered(1)),
            pl.BlockSpec((tm, d), lambda i: (i, 0)),
            pl.BlockSpec((1, d), lambda i: (0, 0)),
            pl.BlockSpec((1, d), lambda i: (0, 0)),
        ],
        out_specs=pl.BlockSpec((tm, d), lambda i: (i, 0)),
        compiler_params=_params("parallel"),
        name="proj_ln",
    )(a, w_bf16, h, g.reshape(1, d), b.reshape(1, d))


def _pool_ln_kernel(h_ref, hp_ref, hn_ref, w_ref, sc_ref, g_ref, b_ref, o_ref, ext_ref, z_ref, *, alpha, seq):
    i = pl.program_id(0)
    tm, d = h_ref.shape
    c = d // len(POOL_WINDOWS)
    ext_ref[0:POOL_HALO, :] = jnp.where(i > 0, hp_ref[...], 0.0)
    ext_ref[POOL_HALO:POOL_HALO + tm, :] = h_ref[...]
    ext_ref[POOL_HALO + tm:2 * POOL_HALO + tm, :] = jnp.where(i < pl.num_programs(0) - 1, hn_ref[...], 0.0)
    t = i * tm + lax.broadcasted_iota(I32, (tm, 1), 0)
    for g, w in enumerate(POOL_WINDOWS):
        half = w // 2
        cols = slice(g * c, (g + 1) * c)
        win = ext_ref[POOL_HALO - half:POOL_HALO - half + tm, cols]
        for j in range(1 - half, half):
            win = win + ext_ref[POOL_HALO + j:POOL_HALO + j + tm, cols]
        cnt = (jnp.minimum(t + half, seq) - jnp.maximum(t - half, 0)).astype(F32)
        x = h_ref[:, cols]
        diff = win / cnt - x
        y = jnp.dot(diff.astype(BF16), w_ref[g], preferred_element_type=F32) * sc_ref[:, cols]
        z_ref[:, cols] = alpha * x + y
    o_ref[...] = _layer_norm(z_ref[...], g_ref[...], b_ref[...])


def _pool_ln(h, w_bf16, scale, g, b, alpha):
    t, d = h.shape
    tm = min(TM_POOL, t)
    hb = tm // POOL_HALO
    n_halo = t // POOL_HALO
    c = d // len(POOL_WINDOWS)
    return pl.pallas_call(
        functools.partial(_pool_ln_kernel, alpha=alpha, seq=t),
        out_shape=jax.ShapeDtypeStruct((t, d), F32),
        grid=(t // tm,),
        in_specs=[
            pl.BlockSpec((tm, d), lambda i: (i, 0)),
            pl.BlockSpec((POOL_HALO, d), lambda i: (jnp.maximum(i * hb - 1, 0), 0)),
            pl.BlockSpec((POOL_HALO, d), lambda i: (jnp.minimum((i + 1) * hb, n_halo - 1), 0)),
            pl.BlockSpec((len(POOL_WINDOWS), c, c), lambda i: (0, 0, 0)),
            pl.BlockSpec((1, d), lambda i: (0, 0)),
            pl.BlockSpec((1, d), lambda i: (0, 0)),
            pl.BlockSpec((1, d), lambda i: (0, 0)),
        ],
        out_specs=pl.BlockSpec((tm, d), lambda i: (i, 0)),
        scratch_shapes=[pltpu.VMEM((tm + 2 * POOL_HALO, d), F32), pltpu.VMEM((tm, d), F32)],
        compiler_params=_params("parallel"),
        name="pool_ln",
    )(h, h, h, w_bf16, scale.reshape(1, d), g.reshape(1, d), b.reshape(1, d))


def _split_bf16(x):
    hi = x.astype(BF16)
    lo = (x - hi.astype(F32)).astype(BF16)
    return hi, lo


def _router_kernel(h_ref, wt_ref, b_ref, idx_ref, gate_ref, rank_ref, cnt_ref, carry_ref, tri_ref):
    tm = h_ref.shape[0]
    n_exp = wt_ref.shape[0]
    n_grp, per = N_EXPERT_GROUPS, EXPERTS_PER_GROUP

    @pl.when(pl.program_id(0) == 0)
    def _():
        carry_ref[...] = jnp.zeros(carry_ref.shape, F32)
        r = lax.broadcasted_iota(I32, (tm, tm), 0)
        c = lax.broadcasted_iota(I32, (tm, tm), 1)
        tri_ref[...] = jnp.where(r <= c, 1.0, 0.0).astype(BF16)

    h_hi, h_lo = _split_bf16(h_ref[...])
    w_hi, w_lo = _split_bf16(wt_ref[...])
    dn = (((1,), (1,)), ((), ()))
    logits = (lax.dot_general(w_hi, h_hi, dn, preferred_element_type=F32)
              + lax.dot_general(w_hi, h_lo, dn, preferred_element_type=F32)
              + lax.dot_general(w_lo, h_hi, dn, preferred_element_type=F32))
    scores = 1.0 / (1.0 + jnp.exp(-logits))
    biased = scores + b_ref[...]

    v = biased.reshape(n_grp, per, tm)
    piota = lax.broadcasted_iota(I32, (n_grp, per, tm), 1)
    m1 = jnp.max(v, axis=1, keepdims=True)
    i1 = jnp.min(jnp.where(v == m1, piota, per), axis=1, keepdims=True)
    v2 = jnp.where(piota == i1, -jnp.inf, v)
    m2 = jnp.max(v2, axis=1, keepdims=True)
    i2 = jnp.min(jnp.where(v2 == m2, piota, per), axis=1, keepdims=True)
    grp_score = m1 + m2
    giota = lax.broadcasted_iota(I32, (n_grp, 1, tm), 0)
    best = jnp.min(jnp.where(grp_score == jnp.max(grp_score, axis=0, keepdims=True), giota, n_grp),
                   axis=0, keepdims=True)
    sel = giota == best
    e0 = jnp.sum(jnp.where(sel, giota * per + i1, 0), axis=0)
    e1 = jnp.sum(jnp.where(sel, giota * per + i2, 0), axis=0)

    eiota = lax.broadcasted_iota(I32, (n_exp, tm), 0)
    oh0 = eiota == e0
    oh1 = eiota == e1
    s0 = jnp.sum(jnp.where(oh0, scores, 0.0), axis=0, keepdims=True)
    s1 = jnp.sum(jnp.where(oh1, scores, 0.0), axis=0, keepdims=True)
    den = s0 + s1

    member = jnp.where(oh0, 1.0, jnp.where(oh1, 1.0, 0.0)).astype(BF16)
    cnt = jnp.dot(member, tri_ref[...], preferred_element_type=F32)
    tot = cnt + carry_ref[:, 0:1]
    r0 = jnp.sum(jnp.where(oh0, tot, 0.0), axis=0, keepdims=True) - 1.0
    r1 = jnp.sum(jnp.where(oh1, tot, 0.0), axis=0, keepdims=True) - 1.0
    carry_ref[...] = carry_ref[...] + cnt[:, tm - 1:tm]

    idx_ref[0:1, :] = e0
    idx_ref[1:2, :] = e1
    gate_ref[0:1, :] = s0 / den
    gate_ref[1:2, :] = s1 / den
    rank_ref[0:1, :] = r0.astype(I32)
    rank_ref[1:2, :] = r1.astype(I32)
    cnt_ref[...] = carry_ref[...]


def _router(h, router_wt, router_b):
    t, d = h.shape
    n_exp = router_wt.shape[0]
    tm = min(TM_ROUTER, t)
    pair = lambda dt: jax.ShapeDtypeStruct((TOP_K, t), dt)
    pair_spec = pl.BlockSpec((TOP_K, tm), lambda i: (0, i))
    return pl.pallas_call(
        _router_kernel,
        out_shape=(pair(I32), pair(F32), pair(I32), jax.ShapeDtypeStruct((n_exp, HEAD_DIM), F32)),
        grid=(t // tm,),
        in_specs=[
            pl.BlockSpec((tm, d), lambda i: (i, 0)),
            pl.BlockSpec((n_exp, d), lambda i: (0, 0)),
            pl.BlockSpec((n_exp, 1), lambda i: (0, 0)),
        ],
        out_specs=(pair_spec, pair_spec, pair_spec, pl.BlockSpec((n_exp, HEAD_DIM), lambda i: (0, 0))),
        scratch_shapes=[pltpu.VMEM((n_exp, HEAD_DIM), F32), pltpu.VMEM((tm, tm), BF16)],
        compiler_params=_params("arbitrary"),
        name="router",
    )(h, router_wt, router_b.reshape(n_exp, 1))


def _pos_kernel(idx_ref, rank_ref, starts_ref, pos_ref):
    n_exp = starts_ref.shape[0]
    tm = idx_ref.shape[1]
    eiota = lax.broadcasted_iota(I32, (n_exp, tm), 0)
    st = starts_ref[...]
    for k in range(TOP_K):
        oh = eiota == idx_ref[k:k + 1, :]
        pos_ref[k:k + 1, :] = jnp.sum(jnp.where(oh, st, 0), axis=0, keepdims=True) + rank_ref[k:k + 1, :]


def _positions(idx, rank, starts):
    t = idx.shape[1]
    n_exp = starts.shape[0]
    tm = min(TM_POS, t)
    pair_spec = pl.BlockSpec((TOP_K, tm), lambda i: (0, i))
    return pl.pallas_call(
        _pos_kernel,
        out_shape=jax.ShapeDtypeStruct((TOP_K, t), I32),
        grid=(t // tm,),
        in_specs=[pair_spec, pair_spec, pl.BlockSpec((n_exp, 1), lambda i: (0, 0))],
        out_specs=pair_spec,
        compiler_params=_params("parallel"),
        name="positions",
    )(idx, rank, starts.reshape(n_exp, 1))


def _dispatch_kernel(pos_ref, h_ref, xs_ref, sem):
    tm = h_ref.shape[0]
    n_tok = pos_ref.shape[0] // TOP_K
    base = pl.program_id(0) * tm

    def row_copy(r, k):
        p = pos_ref[k * n_tok + base + r]
        return pltpu.make_async_copy(h_ref.at[pl.ds(r, 1), :], xs_ref.at[pl.ds(p, 1), :], sem.at[k])

    def issue(r, carry):
        for k in range(TOP_K):
            row_copy(r, k).start()
        return carry

    def drain(r, carry):
        for k in range(TOP_K):
            row_copy(r, k).wait()
        return carry

    lax.fori_loop(0, tm, issue, 0)
    lax.fori_loop(0, tm, drain, 0)


def _dispatch(h, pos_flat):
    t, d = h.shape
    tm = min(TM_DISPATCH, t)
    return pl.pallas_call(
        _dispatch_kernel,
        out_shape=jax.ShapeDtypeStruct((TOP_K * t, d), F32),
        grid_spec=pltpu.PrefetchScalarGridSpec(
            num_scalar_prefetch=1,
            grid=(t // tm,),
            in_specs=[pl.BlockSpec((tm, d), lambda i, pos: (i, 0))],
            out_specs=pl.BlockSpec(memory_space=pl.ANY),
            scratch_shapes=[pltpu.SemaphoreType.DMA((TOP_K,))],
        ),
        compiler_params=_params("arbitrary"),
        name="dispatch",
    )(pos_flat, h)


def _expert_kernel(tile_ref, exp_ref, lo_ref, hi_ref, nw_ref, x_ref, win_ref, wout_ref, o_ref, winb_ref, woutb_ref):
    w = pl.program_id(0)
    tm = x_ref.shape[0]
    d_ff = wout_ref.shape[0]
    prev = jnp.maximum(w - 1, 0)
    new_expert = jnp.logical_or(w == 0, exp_ref[w] != exp_ref[prev])
    new_tile = jnp.logical_or(w == 0, tile_ref[w] != tile_ref[prev])
    active = w < nw_ref[0]

    @pl.when(jnp.logical_and(active, new_expert))
    def _():
        winb_ref[...] = win_ref[...].astype(BF16)
        woutb_ref[...] = wout_ref[...].astype(BF16)

    @pl.when(active)
    def _():
        hid = jnp.dot(x_ref[...].astype(BF16), winb_ref[...], preferred_element_type=F32)
        hg, hu = hid[:, :d_ff], hid[:, d_ff:]
        act = hg * (1.0 / (1.0 + jnp.exp(-hg))) * hu
        y = jnp.dot(act.astype(BF16), woutb_ref[...], preferred_element_type=F32)
        rows = lax.broadcasted_iota(I32, (tm, 1), 0)
        mine = jnp.logical_and(rows >= lo_ref[w], rows < hi_ref[w])

        @pl.when(new_tile)
        def _():
            o_ref[...] = jnp.where(mine, y, 0.0)

        @pl.when(jnp.logical_not(new_tile))
        def _():
            o_ref[...] = jnp.where(mine, y, o_ref[...])


def _work_items(counts, tm, n_rows):
    n_exp = counts.shape[0]
    n_tiles = n_rows // tm
    n_work = n_tiles + n_exp - 1
    ends = jnp.cumsum(counts)
    starts = ends - counts
    first = starts // tm
    last = jnp.maximum(ends - 1, starts) // tm
    n_items = jnp.where(counts > 0, last - first + 1, 0)
    item_end = jnp.cumsum(n_items)
    item_start = item_end - n_items
    total = item_end[-1]
    w = jnp.arange(n_work, dtype=I32)
    wc = jnp.minimum(w, total - 1)
    e = jnp.searchsorted(item_end, wc, side="right").astype(I32)
    tile = first[e] + (wc - item_start[e])
    lo = jnp.maximum(starts[e], tile * tm) - tile * tm
    hi = jnp.minimum(ends[e], (tile + 1) * tm) - tile * tm
    return tile.astype(I32), e, lo.astype(I32), hi.astype(I32), total.reshape(1).astype(I32), starts.astype(I32)


def _experts(xs, w_in, w_out, layer, items):
    p, d = xs.shape
    two_f = w_in.shape[3]
    d_ff = w_out.shape[2]
    tm = min(TM_EXPERT, p)
    tile, e, lo, hi, total = items
    n_work = tile.shape[0]
    return pl.pallas_call(
        _expert_kernel,
        out_shape=jax.ShapeDtypeStruct((p, d), F32),
        grid_spec=pltpu.PrefetchScalarGridSpec(
            num_scalar_prefetch=5,
            grid=(n_work,),
            in_specs=[
                pl.BlockSpec((tm, d), lambda w, t, e, lo, hi, n: (t[w], 0)),
                pl.BlockSpec((None, None, d, two_f), lambda w, t, e, lo, hi, n: (layer, e[w], 0, 0)),
                pl.BlockSpec((None, None, d_ff, d), lambda w, t, e, lo, hi, n: (layer, e[w], 0, 0)),
            ],
            out_specs=pl.BlockSpec((tm, d), lambda w, t, e, lo, hi, n: (t[w], 0)),
            scratch_shapes=[pltpu.VMEM((d, two_f), BF16), pltpu.VMEM((d_ff, d), BF16)],
        ),
        compiler_params=_params("arbitrary"),
        name="experts",
    )(tile, e, lo, hi, total, xs, w_in, w_out)


def _combine_kernel(pos_ref, h_ref, gate_ref, g_ref, b_ref, y_ref, o_ref, ybuf_ref, sem, *, alpha):
    tm = h_ref.shape[0]
    n_tok = pos_ref.shape[0] // TOP_K
    base = pl.program_id(0) * tm

    def row_copy(r, k):
        p = pos_ref[k * n_tok + base + r]
        return pltpu.make_async_copy(y_ref.at[pl.ds(p, 1), :], ybuf_ref.at[k, pl.ds(r, 1), :], sem.at[k])

    def issue(r, carry):
        for k in range(TOP_K):
            row_copy(r, k).start()
        return carry

    def drain(r, carry):
        for k in range(TOP_K):
            row_copy(r, k).wait()
        return carry

    lax.fori_loop(0, tm, issue, 0)
    lax.fori_loop(0, tm, drain, 0)
    gate = gate_ref[...]
    ffn = ybuf_ref[0] * gate[:, 0:1] + ybuf_ref[1] * gate[:, 1:2]
    z = alpha * h_ref[...] + ffn
    o_ref[...] = _layer_norm(z, g_ref[...], b_ref[...])


def _combine_ln(h, y_sorted, pos_flat, gate_tk, g, b, alpha):
    t, d = h.shape
    tm = min(TM_COMBINE, t)
    return pl.pallas_call(
        functools.partial(_combine_kernel, alpha=alpha),
        out_shape=jax.ShapeDtypeStruct((t, d), F32),
        grid_spec=pltpu.PrefetchScalarGridSpec(
            num_scalar_prefetch=1,
            grid=(t // tm,),
            in_specs=[
                pl.BlockSpec((tm, d), lambda i, pos: (i, 0)),
                pl.BlockSpec((tm, TOP_K), lambda i, pos: (i, 0)),
                pl.BlockSpec((1, d), lambda i, pos: (0, 0)),
                pl.BlockSpec((1, d), lambda i, pos: (0, 0)),
                pl.BlockSpec(memory_space=pl.ANY),
            ],
            out_specs=pl.BlockSpec((tm, d), lambda i, pos: (i, 0)),
            scratch_shapes=[pltpu.VMEM((TOP_K, tm, d), F32), pltpu.SemaphoreType.DMA((TOP_K,))],
        ),
        compiler_params=_params("arbitrary"),
        name="combine_ln",
    )(pos_flat, h, gate_tk, g.reshape(1, d), b.reshape(1, d), y_sorted)


def _moe_ln(h, router_wt, router_b, w_in, w_out, layer, g, b, alpha):
    t = h.shape[0]
    idx, gate, rank, cnt = _router(h, router_wt, router_b)
    counts = cnt[:, 0].astype(I32)
    tile, e, lo, hi, total, starts = _work_items(counts, min(TM_EXPERT, TOP_K * t), TOP_K * t)
    pos_flat = _positions(idx, rank, starts).reshape(TOP_K * t)
    xs = _dispatch(h, pos_flat)
    ys = _experts(xs, w_in, w_out, layer, (tile, e, lo, hi, total))
    return _combine_ln(h, ys, pos_flat, gate.T, g, b, alpha)


def kernel(x, w_qkv, lambda_qk, subln_g, w_o, pool_w, pool_scale, router_w, router_b, moe_w_in, moe_w_out,
           ln_g, ln_b):
    batch, seq, d = x.shape
    depth = ln_g.shape[0]
    alpha = (2.0 * depth) ** 0.25
    tables = _rope_tables(seq)
    router_wt = router_w.T
    outs = []
    for bi in range(batch):
        h = x[bi]
        for i in range(depth):
            j = i // N_MIXERS
            if i % N_MIXERS == 0:
                lam_init = 0.8 - 0.6 * math.exp(-0.3 * i)
                qkv = _qkv_proj(h, w_qkv[j].astype(BF16), tables)
                att = _diff_attention(qkv, lambda_qk[j], subln_g[j], lam_init)
                h = _proj_ln(att, w_o[j].astype(BF16), h, ln_g[i, 0], ln_b[i, 0], alpha)
            else:
                h = _pool_ln(h, pool_w[j].astype(BF16), pool_scale[j], ln_g[i, 0], ln_b[i, 0], alpha)
            h = _moe_ln(h, router_wt, router_b, moe_w_in, moe_w_out, i, ln_g[i, 1], ln_b[i, 1], alpha)
        outs.append(h)
    return jnp.stack(outs)
```

```python
import functools
import math

import jax
import jax.numpy as jnp
from jax import lax
from jax.experimental import pallas as pl
from jax.experimental.pallas import tpu as pltpu

F32 = jnp.float32
BF16 = jnp.bfloat16
I32 = jnp.int32

HEAD_DIM = 128
V_HEAD_DIM = 2 * HEAD_DIM
ROT_DIM = HEAD_DIM // 4
ROT_HALF = ROT_DIM // 2
ROPE_THETA = 500000.0
POOL_WINDOWS = (2, 4, 8, 16)
POOL_HALO = 8
N_EXPERT_GROUPS = 8
EXPERTS_PER_GROUP = 8
TOP_K = 2
LN_EPS = 1e-5
RMS_EPS = 1e-5
N_MIXERS = 2
VMEM_LIMIT_BYTES = 56 * 1024 * 1024

TM_QKV, TN_QKV = 1024, 512
TQ_ATTN, TK_ATTN = 512, 1024
Q_TILES_PER_STEP = 8
SHIFT_MARGIN = 1.001
SAFE_MIN_SUM = 2.0 ** -80
TM_PROJ = 512
TM_POOL = 512
TM_ROUTER = 512
TM_POS = 2048
TM_DISPATCH = 256
TM_EXPERT = 256
TM_COMBINE = 256


def _params(*sem):
    return pltpu.CompilerParams(dimension_semantics=sem, vmem_limit_bytes=VMEM_LIMIT_BYTES)


def _layer_norm(z, g, b):
    mu = jnp.mean(z, axis=-1, keepdims=True)
    zc = z - mu
    var = jnp.mean(zc * zc, axis=-1, keepdims=True)
    return zc * lax.rsqrt(var + LN_EPS) * g + b


def _rope_tables(seq):
    pos = jnp.arange(seq, dtype=F32)
    inv_freq = ROPE_THETA ** (-jnp.arange(0, ROT_DIM, 2, dtype=F32) / ROT_DIM)
    ang = pos[:, None] * inv_freq[None, :]
    cos, sin = jnp.cos(ang), jnp.sin(ang)
    rest = HEAD_DIM - ROT_DIM
    z_half = jnp.zeros((seq, ROT_HALF), F32)
    z_rest = jnp.zeros((seq, rest), F32)
    a = jnp.concatenate([cos, cos, jnp.ones((seq, rest), F32)], axis=1)
    b = jnp.concatenate([z_half, sin, z_rest], axis=1)
    c = jnp.concatenate([-sin, z_half, z_rest], axis=1)
    rot = jnp.stack([a, b, c])
    ident = jnp.stack([jnp.ones((seq, HEAD_DIM), F32), jnp.zeros((seq, HEAD_DIM), F32),
                       jnp.zeros((seq, HEAD_DIM), F32)])
    return jnp.stack([rot * (HEAD_DIM ** -0.5 * math.log2(math.e)), rot, ident])


def _qkv_kernel(x_ref, w_ref, tab_ref, o_ref, xb_ref):
    @pl.when(pl.program_id(1) == 0)
    def _():
        xb_ref[...] = x_ref[...].astype(BF16)

    y = jnp.dot(xb_ref[...], w_ref[...], preferred_element_type=F32)
    a, b, c = tab_ref[0], tab_ref[1], tab_ref[2]
    for j in range(y.shape[1] // HEAD_DIM):
        sl = slice(j * HEAD_DIM, (j + 1) * HEAD_DIM)
        yj = y[:, sl]
        out = yj * a + pltpu.roll(yj, ROT_HALF, 1) * b + pltpu.roll(yj, HEAD_DIM - ROT_HALF, 1) * c
        o_ref[:, sl] = out.astype(o_ref.dtype)


def _qkv_proj(h, w_bf16, tables):
    t, d = h.shape
    n = w_bf16.shape[1]
    tm, tn = min(TM_QKV, t), min(TN_QKV, d)
    return pl.pallas_call(
        _qkv_kernel,
        out_shape=jax.ShapeDtypeStruct((t, n), BF16),
        grid=(t // tm, n // tn),
        in_specs=[
            pl.BlockSpec((tm, d), lambda m, j: (m, 0)),
            pl.BlockSpec((d, tn), lambda m, j: (0, j)),
            pl.BlockSpec((None, 3, tm, HEAD_DIM), lambda m, j: ((j * tn) // d, 0, m, 0)),
        ],
        out_specs=pl.BlockSpec((tm, tn), lambda m, j: (m, j)),
        scratch_shapes=[pltpu.VMEM((tm, d), BF16)],
        compiler_params=_params("parallel", "arbitrary"),
        name="qkv_rope",
    )(h, w_bf16, tables)


def _attn_kernel(lam_ref, g_ref, q_ref, k_ref, v_ref, o_ref,
                 s0_ref, s1_ref, p0_ref, p1_ref, al0_ref, al1_ref,
                 m_ref, l_ref, acc_ref, *, tq, tk, lam_init):
    n = k_ref.shape[0] // tk
    n_tiles = q_ref.shape[0] // tq
    s_refs, p_refs = (s0_ref, s1_ref), (p0_ref, p1_ref)
    al_refs = (al0_ref, al1_ref)

    lq = lam_ref[...]
    lam = (jnp.exp(jnp.sum(lq[0:1] * lq[1:2], axis=1, keepdims=True))
           - jnp.exp(jnp.sum(lq[2:3] * lq[3:4], axis=1, keepdims=True)) + lam_init)

    def rows(u):
        return pl.ds(pl.multiple_of(u * tq, tq), tq)

    def init(st):
        m_ref[st] = jnp.full(m_ref.shape[1:], -jnp.inf, F32)
        l_ref[st] = jnp.zeros(l_ref.shape[1:], F32)
        acc_ref[st] = jnp.zeros(acc_ref.shape[1:], F32)

    def scores(u, j, slot):
        start = pl.multiple_of(j * tk, tk)
        for c in range(2):
            cols = slice(c * HEAD_DIM, (c + 1) * HEAD_DIM)
            s = lax.dot_general(q_ref[rows(u), cols], k_ref[pl.ds(start, tk), cols],
                                (((1,), (1,)), ((), ())), preferred_element_type=F32)
            s_refs[slot][c] = s

    def probs(st, slot):
        for c in range(2):
            m_prev = m_ref[st, c]
            m_new = jnp.maximum(m_prev, jnp.max(s_refs[slot][c], axis=1, keepdims=True))
            alpha = jnp.exp2(m_prev - m_new)
            p = jnp.exp2(s_refs[slot][c] - m_new)
            l_ref[st, c] = alpha * l_ref[st, c] + jnp.sum(p, axis=1, keepdims=True)
            p_refs[slot][c] = p.astype(BF16)
            al_refs[slot][c] = alpha
            m_ref[st, c] = m_new

    def accumulate(st, j, slot):
        start = pl.multiple_of(j * tk, tk)
        vs = v_ref[pl.ds(start, tk), :]
        for c in range(2):
            acc_ref[st, c] = (al_refs[slot][c] * acc_ref[st, c]
                              + jnp.dot(p_refs[slot][c], vs, preferred_element_type=F32))

    def finalize(st, u):
        o = acc_ref[st, 0] / l_ref[st, 0] - lam * (acc_ref[st, 1] / l_ref[st, 1])
        ms = jnp.mean(o * o, axis=1, keepdims=True)
        of = o * lax.rsqrt(ms + RMS_EPS)
        of = of * g_ref[...] * (1.0 - lam_init)
        o_ref[rows(u), :] = of.astype(o_ref.dtype)

    def middle(st, u):
        def body(i, carry):
            j = 2 * i + 1
            scores(u, j + 1, 0)
            probs(st, 1)
            accumulate(st, j - 1, 0)
            scores(u, j + 2, 1)
            probs(st, 0)
            accumulate(st, j, 1)
            return carry

        lax.fori_loop(0, (n - 2) // 2, body, 0)

    if n == 1:
        def single(u, carry):
            init(0)
            scores(u, 0, 0)
            probs(0, 0)
            accumulate(0, 0, 0)
            finalize(0, u)
            return carry

        lax.fori_loop(0, n_tiles, single, 0)
        return

    init(0)
    scores(0, 0, 0)
    probs(0, 0)
    scores(0, 1, 1)
    middle(0, 0)
    accumulate(0, n - 2, 0)
    probs(0, 1)
    if n_tiles > 1:
        scores(1, 0, 0)

        def tile(u, carry):
            st = u % 2
            prev = 1 - st
            init(st)
            accumulate(prev, n - 1, 1)
            probs(st, 0)
            scores(u, 1, 1)
            finalize(prev, u - 1)
            middle(st, u)
            accumulate(st, n - 2, 0)
            probs(st, 1)
            scores(jnp.minimum(u + 1, n_tiles - 1), 0, 0)
            return carry

        lax.fori_loop(1, n_tiles, tile, 0)
    st_last = (n_tiles - 1) % 2
    accumulate(st_last, n - 1, 1)
    finalize(st_last, n_tiles - 1)


def _bound_attn_kernel(lam_ref, g_ref, q_ref, k_ref, v_ref, o_ref,
                       p0_ref, p1_ref, kmax_ref, m_ref, ls_ref, acc_ref, *, tq, tk, lam_init):
    n = k_ref.shape[0] // tk
    n_tiles = q_ref.shape[0] // tq
    p_refs = (p0_ref, p1_ref)
    comp = lambda c: slice(c * HEAD_DIM, (c + 1) * HEAD_DIM)
    nt_dims = (((1,), (1,)), ((), ()))

    lq = lam_ref[...]
    lam = (jnp.exp(jnp.sum(lq[0:1] * lq[1:2], axis=1, keepdims=True))
           - jnp.exp(jnp.sum(lq[2:3] * lq[3:4], axis=1, keepdims=True)) + lam_init)

    @pl.when(pl.program_id(1) == 0)
    def _():
        def chunk(i, carry):
            kk = k_ref[pl.ds(pl.multiple_of(i * tk, tk), tk), :].astype(F32)
            sq = kk * kk
            return tuple(jnp.maximum(carry[c], jnp.max(jnp.sum(sq[:, comp(c)], axis=1, keepdims=True),
                                                       axis=0, keepdims=True)) for c in range(2))

        mx = lax.fori_loop(0, n, chunk, (jnp.zeros((1, 1), F32), jnp.zeros((1, 1), F32)))
        for c in range(2):
            kmax_ref[c] = jnp.broadcast_to(jnp.sqrt(mx[c]), kmax_ref.shape[1:])

    def rows(u):
        return pl.ds(pl.multiple_of(u * tq, tq), tq)

    def setup(st, u):
        for c in range(2):
            qc = q_ref[rows(u), comp(c)].astype(F32)
            qn = jnp.sqrt(jnp.sum(qc * qc, axis=1, keepdims=True))
            bound = qn * kmax_ref[c][0:1, 0:1] * SHIFT_MARGIN
            m_ref[st, c] = jnp.broadcast_to(bound, (tq, HEAD_DIM))
        ls_ref[st] = jnp.zeros(ls_ref.shape[1:], F32)
        acc_ref[st] = jnp.zeros(acc_ref.shape[1:], F32)

    def expo(st, u, j, slot):
        start = pl.multiple_of(j * tk, tk)
        for c in range(2):
            s = lax.dot_general(q_ref[rows(u), comp(c)], k_ref[pl.ds(start, tk), comp(c)], nt_dims,
                                preferred_element_type=F32)
            shift = m_ref[st, c]
            ls = ls_ref[st, c]
            for b in range(tk // HEAD_DIM):
                lanes = slice(b * HEAD_DIM, (b + 1) * HEAD_DIM)
                p = jnp.exp2(s[:, lanes] - shift)
                ls = ls + p
                p_refs[slot][c, :, lanes] = p.astype(BF16)
            ls_ref[st, c] = ls

    def accumulate(st, j, slot):
        start = pl.multiple_of(j * tk, tk)
        vs = v_ref[pl.ds(start, tk), :]
        for c in range(2):
            acc_ref[st, c] = acc_ref[st, c] + jnp.dot(p_refs[slot][c], vs, preferred_element_type=F32)

    def write_out(o, u):
        ms = jnp.mean(o * o, axis=1, keepdims=True)
        of = o * lax.rsqrt(ms + RMS_EPS)
        of = of * g_ref[...] * (1.0 - lam_init)
        o_ref[rows(u), :] = of.astype(o_ref.dtype)

    def safe_tile(u):
        outs = []
        for c in range(2):
            def body(j, carry):
                m_prev, l_prev, acc = carry
                start = pl.multiple_of(j * tk, tk)
                s = lax.dot_general(q_ref[rows(u), comp(c)], k_ref[pl.ds(start, tk), comp(c)], nt_dims,
                                    preferred_element_type=F32)
                m_new = jnp.maximum(m_prev, jnp.max(s, axis=1, keepdims=True))
                alpha = jnp.exp2(m_prev - m_new)
                p = jnp.exp2(s - m_new)
                l_new = alpha * l_prev + jnp.sum(p, axis=1, keepdims=True)
                acc = alpha * acc + jnp.dot(p.astype(BF16), v_ref[pl.ds(start, tk), :],
                                            preferred_element_type=F32)
                return m_new, l_new, acc

            init = (jnp.full((tq, 1), -jnp.inf, F32), jnp.zeros((tq, 1), F32),
                    jnp.zeros((tq, V_HEAD_DIM), F32))
            _, l_fin, acc = lax.fori_loop(0, n, body, init)
            outs.append(acc / l_fin)
        write_out(outs[0] - lam * outs[1], u)

    def finalize(st, u):
        l0 = jnp.sum(ls_ref[st, 0], axis=1, keepdims=True)
        l1 = jnp.sum(ls_ref[st, 1], axis=1, keepdims=True)
        healthy = jnp.min(jnp.minimum(l0, l1)) >= SAFE_MIN_SUM

        @pl.when(healthy)
        def _():
            write_out(acc_ref[st, 0] / l0 - lam * (acc_ref[st, 1] / l1), u)

        @pl.when(jnp.logical_not(healthy))
        def _():
            safe_tile(u)

    if n == 1:
        def single(u, carry):
            setup(0, u)
            expo(0, u, 0, 0)
            accumulate(0, 0, 0)
            finalize(0, u)
            return carry

        lax.fori_loop(0, n_tiles, single, 0)
        return

    setup(0, 0)
    expo(0, 0, 0, 0)

    def tile(u, carry):
        st = u % 2
        nxt = jnp.minimum(u + 1, n_tiles - 1)

        def pair(i, c2):
            j = 2 * i
            expo(st, u, j + 1, 1)
            accumulate(st, j, 0)
            expo(st, u, j + 2, 0)
            accumulate(st, j + 1, 1)
            return c2

        lax.fori_loop(0, n // 2 - 1, pair, 0)
        expo(st, u, n - 1, 1)
        accumulate(st, n - 2, 0)
        setup(1 - st, nxt)
        expo(1 - st, nxt, 0, 0)
        accumulate(st, n - 1, 1)
        finalize(st, u)
        return carry

    lax.fori_loop(0, n_tiles, tile, 0)


def _diff_attention(qkv, lam_qk, subln_g, lam_init):
    t = qkv.shape[0]
    d = qkv.shape[1] // 3
    n_heads = d // V_HEAD_DIM
    tq, tk = min(TQ_ATTN, t), min(TK_ATTN, t)
    tq_step = min(tq * Q_TILES_PER_STEP, t)
    assert t // tk == 1 or (t // tk) % 2 == 0
    kern = functools.partial(_bound_attn_kernel, tq=tq, tk=tk, lam_init=lam_init)
    state = pltpu.VMEM((2, 2, tq, HEAD_DIM), F32)
    return pl.pallas_call(
        kern,
        out_shape=jax.ShapeDtypeStruct((t, d), BF16),
        grid=(n_heads, t // tq_step),
        in_specs=[
            pl.BlockSpec((4, HEAD_DIM), lambda h, i: (0, 0)),
            pl.BlockSpec((1, V_HEAD_DIM), lambda h, i: (0, 0)),
            pl.BlockSpec((tq_step, V_HEAD_DIM), lambda h, i: (i, h)),
            pl.BlockSpec((t, V_HEAD_DIM), lambda h, i: (0, n_heads + h), pipeline_mode=pl.Buffered(1)),
            pl.BlockSpec((t, V_HEAD_DIM), lambda h, i: (0, 2 * n_heads + h), pipeline_mode=pl.Buffered(1)),
        ],
        out_specs=pl.BlockSpec((tq_step, V_HEAD_DIM), lambda h, i: (i, h)),
        scratch_shapes=[
            pltpu.VMEM((2, tq, tk), BF16), pltpu.VMEM((2, tq, tk), BF16),
            pltpu.VMEM((2, 8, HEAD_DIM), F32),
            state, state,
            pltpu.VMEM((2, 2, tq, V_HEAD_DIM), F32),
        ],
        compiler_params=_params("arbitrary", "arbitrary"),
        name="diff_attn",
    )(lam_qk, subln_g.reshape(1, V_HEAD_DIM), qkv, qkv, qkv)


def _diff_attention_twopass(qkv, lam_qk, subln_g, lam_init):
    t = qkv.shape[0]
    d = qkv.shape[1] // 3
    n_heads = d // V_HEAD_DIM
    tq, tk = min(TQ_ATTN, t), min(TK_ATTN, t)
    tq_step = min(tq * Q_TILES_PER_STEP, t)
    assert t // tk == 1 or (t // tk) % 2 == 0
    kern = functools.partial(_attn_kernel, tq=tq, tk=tk, lam_init=lam_init)
    slab = lambda dt: pltpu.VMEM((2, tq, tk), dt)
    stat = pltpu.VMEM((2, tq, 1), F32)
    state = pltpu.VMEM((2, 2, tq, 1), F32)
    return pl.pallas_call(
        kern,
        out_shape=jax.ShapeDtypeStruct((t, d), BF16),
        grid=(n_heads, t // tq_step),
        in_specs=[
            pl.BlockSpec((4, HEAD_DIM), lambda h, i: (0, 0)),
            pl.BlockSpec((1, V_HEAD_DIM), lambda h, i: (0, 0)),
            pl.BlockSpec((tq_step, V_HEAD_DIM), lambda h, i: (i, h)),
            pl.BlockSpec((t, V_HEAD_DIM), lambda h, i: (0, n_heads + h), pipeline_mode=pl.Buffered(1)),
            pl.BlockSpec((t, V_HEAD_DIM), lambda h, i: (0, 2 * n_heads + h), pipeline_mode=pl.Buffered(1)),
        ],
        out_specs=pl.BlockSpec((tq_step, V_HEAD_DIM), lambda h, i: (i, h)),
        scratch_shapes=[
            slab(F32), slab(F32), slab(BF16), slab(BF16),
            stat, stat,
            state, state,
            pltpu.VMEM((2, 2, tq, V_HEAD_DIM), F32),
        ],
        compiler_params=_params("parallel", "parallel"),
        name="diff_attn",
    )(lam_qk, subln_g.reshape(1, V_HEAD_DIM), qkv, qkv, qkv)


def _proj_ln_kernel(a_ref, w_ref, h_ref, g_ref, b_ref, o_ref, *, alpha):
    mix = jnp.dot(a_ref[...], w_ref[...], preferred_element_type=F32)
    z = alpha * h_ref[...] + mix
    o_ref[...] = _layer_norm(z, g_ref[...], b_ref[...])


def _proj_ln(a, w_bf16, h, g, b, alpha):
    t, d = h.shape
    tm = min(TM_PROJ, t)
    return pl.pallas_call(
        functools.partial(_proj_ln_kernel, alpha=alpha),
        out_shape=jax.ShapeDtypeStruct((t, d), F32),
        grid=(t // tm,),
        in_specs=[
            pl.BlockSpec((tm, d), lambda i: (i, 0)),
            pl.BlockSpec((d, d), lambda i: (0, 0), pipeline_mode=pl.Buffered(1)),
            pl.BlockSpec((tm, d), lambda i: (i, 0)),
            pl.BlockSpec((1, d), lambda i: (0, 0)),
            pl.BlockSpec((1, d), lambda i: (0, 0)),
        ],
        out_specs=pl.BlockSpec((tm, d), lambda i: (i, 0)),
        compiler_params=_params("parallel"),
        name="proj_ln",
    )(a, w_bf16, h, g.reshape(1, d), b.reshape(1, d))


def _pool_ln_kernel(h_ref, hp_ref, hn_ref, w_ref, sc_ref, g_ref, b_ref, o_ref, ext_ref, z_ref, *, alpha, seq):
    i = pl.program_id(0)
    tm, d = h_ref.shape
    c = d // len(POOL_WINDOWS)
    ext_ref[0:POOL_HALO, :] = jnp.where(i > 0, hp_ref[...], 0.0)
    ext_ref[POOL_HALO:POOL_HALO + tm, :] = h_ref[...]
    ext_ref[POOL_HALO + tm:2 * POOL_HALO + tm, :] = jnp.where(i < pl.num_programs(0) - 1, hn_ref[...], 0.0)
    t = i * tm + lax.broadcasted_iota(I32, (tm, 1), 0)
    for g, w in enumerate(POOL_WINDOWS):
        half = w // 2
        cols = slice(g * c, (g + 1) * c)
        win = ext_ref[POOL_HALO - half:POOL_HALO - half + tm, cols]
        for j in range(1 - half, half):
            win = win + ext_ref[POOL_HALO + j:POOL_HALO + j + tm, cols]
        cnt = (jnp.minimum(t + half, seq) - jnp.maximum(t - half, 0)).astype(F32)
        x = h_ref[:, cols]
        diff = win / cnt - x
        y = jnp.dot(diff.astype(BF16), w_ref[g], preferred_element_type=F32) * sc_ref[:, cols]
        z_ref[:, cols] = alpha * x + y
    o_ref[...] = _layer_norm(z_ref[...], g_ref[...], b_ref[...])


def _pool_ln(h, w_bf16, scale, g, b, alpha):
    t, d = h.shape
    tm = min(TM_POOL, t)
    hb = tm // POOL_HALO
    n_halo = t // POOL_HALO
    c = d // len(POOL_WINDOWS)
    return pl.pallas_call(
        functools.partial(_pool_ln_kernel, alpha=alpha, seq=t),
        out_shape=jax.ShapeDtypeStruct((t, d), F32),
        grid=(t // tm,),
        in_specs=[
            pl.BlockSpec((tm, d), lambda i: (i, 0)),
            pl.BlockSpec((POOL_HALO, d), lambda i: (jnp.maximum(i * hb - 1, 0), 0)),
            pl.BlockSpec((POOL_HALO, d), lambda i: (jnp.minimum((i + 1) * hb, n_halo - 1), 0)),
            pl.BlockSpec((len(POOL_WINDOWS), c, c), lambda i: (0, 0, 0)),
            pl.BlockSpec((1, d), lambda i: (0, 0)),
            pl.BlockSpec((1, d), lambda i: (0, 0)),
            pl.BlockSpec((1, d), lambda i: (0, 0)),
        ],
        out_specs=pl.BlockSpec((tm, d), lambda i: (i, 0)),
        scratch_shapes=[pltpu.VMEM((tm + 2 * POOL_HALO, d), F32), pltpu.VMEM((tm, d), F32)],
        compiler_params=_params("parallel"),
        name="pool_ln",
    )(h, h, h, w_bf16, scale.reshape(1, d), g.reshape(1, d), b.reshape(1, d))


def _split_bf16(x):
    hi = x.astype(BF16)
    lo = (x - hi.astype(F32)).astype(BF16)
    return hi, lo


def _router_kernel(h_ref, wt_ref, b_ref, idx_ref, gate_ref, rank_ref, cnt_ref, carry_ref, tri_ref):
    tm = h_ref.shape[0]
    n_exp = wt_ref.shape[0]
    n_grp, per = N_EXPERT_GROUPS, EXPERTS_PER_GROUP

    @pl.when(pl.program_id(0) == 0)
    def _():
        carry_ref[...] = jnp.zeros(carry_ref.shape, F32)
        r = lax.broadcasted_iota(I32, (tm, tm), 0)
        c = lax.broadcasted_iota(I32, (tm, tm), 1)
        tri_ref[...] = jnp.where(r <= c, 1.0, 0.0).astype(BF16)

    h_hi, h_lo = _split_bf16(h_ref[...])
    w_hi, w_lo = _split_bf16(wt_ref[...])
    dn = (((1,), (1,)), ((), ()))
    logits = (lax.dot_general(w_hi, h_hi, dn, preferred_element_type=F32)
              + lax.dot_general(w_hi, h_lo, dn, preferred_element_type=F32)
              + lax.dot_general(w_lo, h_hi, dn, preferred_element_type=F32))
    scores = 1.0 / (1.0 + jnp.exp(-logits))
    biased = scores + b_ref[...]

    v = biased.reshape(n_grp, per, tm)
    piota = lax.broadcasted_iota(I32, (n_grp, per, tm), 1)
    m1 = jnp.max(v, axis=1, keepdims=True)
    i1 = jnp.min(jnp.where(v == m1, piota, per), axis=1, keepdims=True)
    v2 = jnp.where(piota == i1, -jnp.inf, v)
    m2 = jnp.max(v2, axis=1, keepdims=True)
    i2 = jnp.min(jnp.where(v2 == m2, piota, per), axis=1, keepdims=True)
    grp_score = m1 + m2
    giota = lax.broadcasted_iota(I32, (n_grp, 1, tm), 0)
    best = jnp.min(jnp.where(grp_score == jnp.max(grp_score, axis=0, keepdims=True), giota, n_grp),
                   axis=0, keepdims=True)
    sel = giota == best
    e0 = jnp.sum(jnp.where(sel, giota * per + i1, 0), axis=0)
    e1 = jnp.sum(jnp.where(sel, giota * per + i2, 0), axis=0)

    eiota = lax.broadcasted_iota(I32, (n_exp, tm), 0)
    oh0 = eiota == e0
    oh1 = eiota == e1
    s0 = jnp.sum(jnp.where(oh0, scores, 0.0), axis=0, keepdims=True)
    s1 = jnp.sum(jnp.where(oh1, scores, 0.0), axis=0, keepdims=True)
    den = s0 + s1

    member = jnp.where(oh0, 1.0, jnp.where(oh1, 1.0, 0.0)).astype(BF16)
    cnt = jnp.dot(member, tri_ref[...], preferred_element_type=F32)
    tot = cnt + carry_ref[:, 0:1]
    r0 = jnp.sum(jnp.where(oh0, tot, 0.0), axis=0, keepdims=True) - 1.0
    r1 = jnp.sum(jnp.where(oh1, tot, 0.0), axis=0, keepdims=True) - 1.0
    carry_ref[...] = carry_ref[...] + cnt[:, tm - 1:tm]

    idx_ref[0:1, :] = e0
    idx_ref[1:2, :] = e1
    gate_ref[0:1, :] = s0 / den
    gate_ref[1:2, :] = s1 / den
    rank_ref[0:1, :] = r0.astype(I32)
    rank_ref[1:2, :] = r1.astype(I32)
    cnt_ref[...] = carry_ref[...]


def _router(h, router_wt, router_b):
    t, d = h.shape
    n_exp = router_wt.shape[0]
    tm = min(TM_ROUTER, t)
    pair = lambda dt: jax.ShapeDtypeStruct((TOP_K, t), dt)
    pair_spec = pl.BlockSpec((TOP_K, tm), lambda i: (0, i))
    return pl.pallas_call(
        _router_kernel,
        out_shape=(pair(I32), pair(F32), pair(I32), jax.ShapeDtypeStruct((n_exp, HEAD_DIM), F32)),
        grid=(t // tm,),
        in_specs=[
            pl.BlockSpec((tm, d), lambda i: (i, 0)),
            pl.BlockSpec((n_exp, d), lambda i: (0, 0)),
            pl.BlockSpec((n_exp, 1), lambda i: (0, 0)),
        ],
        out_specs=(pair_spec, pair_spec, pair_spec, pl.BlockSpec((n_exp, HEAD_DIM), lambda i: (0, 0))),
        scratch_shapes=[pltpu.VMEM((n_exp, HEAD_DIM), F32), pltpu.VMEM((tm, tm), BF16)],
        compiler_params=_params("arbitrary"),
        name="router",
    )(h, router_wt, router_b.reshape(n_exp, 1))


def _pos_kernel(idx_ref, rank_ref, starts_ref, pos_ref):
    n_exp = starts_ref.shape[0]
    tm = idx_ref.shape[1]
    eiota = lax.broadcasted_iota(I32, (n_exp, tm), 0)
    st = starts_ref[...]
    for k in range(TOP_K):
        oh = eiota == idx_ref[k:k + 1, :]
        pos_ref[k:k + 1, :] = jnp.sum(jnp.where(oh, st, 0), axis=0, keepdims=True) + rank_ref[k:k + 1, :]


def _positions(idx, rank, starts):
    t = idx.shape[1]
    n_exp = starts.shape[0]
    tm = min(TM_POS, t)
    pair_spec = pl.BlockSpec((TOP_K, tm), lambda i: (0, i))
    return pl.pallas_call(
        _pos_kernel,
        out_shape=jax.ShapeDtypeStruct((TOP_K, t), I32),
        grid=(t // tm,),
        in_specs=[pair_spec, pair_spec, pl.BlockSpec((n_exp, 1), lambda i: (0, 0))],
        out_specs=pair_spec,
        compiler_params=_params("parallel"),
        name="positions",
    )(idx, rank, starts.reshape(n_exp, 1))


def _dispatch_kernel(pos_ref, h_ref, xs_ref, sem):
    tm = h_ref.shape[0]
    n_tok = pos_ref.shape[0] // TOP_K
    base = pl.program_id(0) * tm

    def row_copy(r, k):
        p = pos_ref[k * n_tok + base + r]
        return pltpu.make_async_copy(h_ref.at[pl.ds(r, 1), :], xs_ref.at[pl.ds(p, 1), :], sem.at[k])

    def issue(r, carry):
        for k in range(TOP_K):
            row_copy(r, k).start()
        return carry

    def drain(r, carry):
        for k in range(TOP_K):
            row_copy(r, k).wait()
        return carry

    lax.fori_loop(0, tm, issue, 0)
    lax.fori_loop(0, tm, drain, 0)


def _dispatch(h, pos_flat):
    t, d = h.shape
    tm = min(TM_DISPATCH, t)
    return pl.pallas_call(
        _dispatch_kernel,
        out_shape=jax.ShapeDtypeStruct((TOP_K * t, d), F32),
        grid_spec=pltpu.PrefetchScalarGridSpec(
            num_scalar_prefetch=1,
            grid=(t // tm,),
            in_specs=[pl.BlockSpec((tm, d), lambda i, pos: (i, 0))],
            out_specs=pl.BlockSpec(memory_space=pl.ANY),
            scratch_shapes=[pltpu.SemaphoreType.DMA((TOP_K,))],
        ),
        compiler_params=_params("arbitrary"),
        name="dispatch",
    )(pos_flat, h)


def _expert_kernel(tile_ref, exp_ref, lo_ref, hi_ref, nw_ref, x_ref, win_ref, wout_ref, o_ref, winb_ref, woutb_ref):
    w = pl.program_id(0)
    tm = x_ref.shape[0]
    d_ff = wout_ref.shape[0]
    prev = jnp.maximum(w - 1, 0)
    new_expert = jnp.logical_or(w == 0, exp_ref[w] != exp_ref[prev])
    new_tile = jnp.logical_or(w == 0, tile_ref[w] != tile_ref[prev])
    active = w < nw_ref[0]

    @pl.when(jnp.logical_and(active, new_expert))
    def _():
        winb_ref[...] = win_ref[...].astype(BF16)
        woutb_ref[...] = wout_ref[...].astype(BF16)

    @pl.when(active)
    def _():
        hid = jnp.dot(x_ref[...].astype(BF16), winb_ref[...], preferred_element_type=F32)
        hg, hu = hid[:, :d_ff], hid[:, d_ff:]
        act = hg * (1.0 / (1.0 + jnp.exp(-hg))) * hu
        y = jnp.dot(act.astype(BF16), woutb_ref[...], preferred_element_type=F32)
        rows = lax.broadcasted_iota(I32, (tm, 1), 0)
        mine = jnp.logical_and(rows >= lo_ref[w], rows < hi_ref[w])

        @pl.when(new_tile)
        def _():
            o_ref[...] = jnp.where(mine, y, 0.0)

        @pl.when(jnp.logical_not(new_tile))
        def _():
            o_ref[...] = jnp.where(mine, y, o_ref[...])


def _work_items(counts, tm, n_rows):
    n_exp = counts.shape[0]
    n_tiles = n_rows // tm
    n_work = n_tiles + n_exp - 1
    ends = jnp.cumsum(counts)
    starts = ends - counts
    first = starts // tm
    last = jnp.maximum(ends - 1, starts) // tm
    n_items = jnp.where(counts > 0, last - first + 1, 0)
    item_end = jnp.cumsum(n_items)
    item_start = item_end - n_items
    total = item_end[-1]
    w = jnp.arange(n_work, dtype=I32)
    wc = jnp.minimum(w, total - 1)
    e = jnp.searchsorted(item_end, wc, side="right").astype(I32)
    tile = first[e] + (wc - item_start[e])
    lo = jnp.maximum(starts[e], tile * tm) - tile * tm
    hi = jnp.minimum(ends[e], (tile + 1) * tm) - tile * tm
    return tile.astype(I32), e, lo.astype(I32), hi.astype(I32), total.reshape(1).astype(I32), starts.astype(I32)


def _experts(xs, w_in, w_out, layer, items):
    p, d = xs.shape
    two_f = w_in.shape[3]
    d_ff = w_out.shape[2]
    tm = min(TM_EXPERT, p)
    tile, e, lo, hi, total = items
    n_work = tile.shape[0]
    return pl.pallas_call(
        _expert_kernel,
        out_shape=jax.ShapeDtypeStruct((p, d), F32),
        grid_spec=pltpu.PrefetchScalarGridSpec(
            num_scalar_prefetch=5,
            grid=(n_work,),
            in_specs=[
                pl.BlockSpec((tm, d), lambda w, t, e, lo, hi, n: (t[w], 0)),
                pl.BlockSpec((None, None, d, two_f), lambda w, t, e, lo, hi, n: (layer, e[w], 0, 0)),
                pl.BlockSpec((None, None, d_ff, d), lambda w, t, e, lo, hi, n: (layer, e[w], 0, 0)),
            ],
            out_specs=pl.BlockSpec((tm, d), lambda w, t, e, lo, hi, n: (t[w], 0)),
            scratch_shapes=[pltpu.VMEM((d, two_f), BF16), pltpu.VMEM((d_ff, d), BF16)],
        ),
        compiler_params=_params("arbitrary"),
        name="experts",
    )(tile, e, lo, hi, total, xs, w_in, w_out)


def _combine_kernel(pos_ref, h_ref, gate_ref, g_ref, b_ref, y_ref, o_ref, ybuf_ref, sem, *, alpha):
    tm = h_ref.shape[0]
    n_tok = pos_ref.shape[0] // TOP_K
    base = pl.program_id(0) * tm

    def row_copy(r, k):
        p = pos_ref[k * n_tok + base + r]
        return pltpu.make_async_copy(y_ref.at[pl.ds(p, 1), :], ybuf_ref.at[k, pl.ds(r, 1), :], sem.at[k])

    def issue(r, carry):
        for k in range(TOP_K):
            row_copy(r, k).start()
        return carry

    def drain(r, carry):
        for k in range(TOP_K):
            row_copy(r, k).wait()
        return carry

    lax.fori_loop(0, tm, issue, 0)
    lax.fori_loop(0, tm, drain, 0)
    gate = gate_ref[...]
    ffn = ybuf_ref[0] * gate[:, 0:1] + ybuf_ref[1] * gate[:, 1:2]
    z = alpha * h_ref[...] + ffn
    o_ref[...] = _layer_norm(z, g_ref[...], b_ref[...])


def _combine_ln(h, y_sorted, pos_flat, gate_tk, g, b, alpha):
    t, d = h.shape
    tm = min(TM_COMBINE, t)
    return pl.pallas_call(
        functools.partial(_combine_kernel, alpha=alpha),
        out_shape=jax.ShapeDtypeStruct((t, d), F32),
        grid_spec=pltpu.PrefetchScalarGridSpec(
            num_scalar_prefetch=1,
            grid=(t // tm,),
            in_specs=[
                pl.BlockSpec((tm, d), lambda i, pos: (i, 0)),
                pl.BlockSpec((tm, TOP_K), lambda i, pos: (i, 0)),
                pl.BlockSpec((1, d), lambda i, pos: (0, 0)),
                pl.BlockSpec((1, d), lambda i, pos: (0, 0)),
                pl.BlockSpec(memory_space=pl.ANY),
            ],
            out_specs=pl.BlockSpec((tm, d), lambda i, pos: (i, 0)),
            scratch_shapes=[pltpu.VMEM((TOP_K, tm, d), F32), pltpu.SemaphoreType.DMA((TOP_K,))],
        ),
        compiler_params=_params("arbitrary"),
        name="combine_ln",
    )(pos_flat, h, gate_tk, g.reshape(1, d), b.reshape(1, d), y_sorted)


def _moe_ln(h, router_wt, router_b, w_in, w_out, layer, g, b, alpha):
    t = h.shape[0]
    idx, gate, rank, cnt = _router(h, router_wt, router_b)
    counts = cnt[:, 0].astype(I32)
    tile, e, lo, hi, total, starts = _work_items(counts, min(TM_EXPERT, TOP_K * t), TOP_K * t)
    pos_flat = _positions(idx, rank, starts).reshape(TOP_K * t)
    xs = _dispatch(h, pos_flat)
    ys = _experts(xs, w_in, w_out, layer, (tile, e, lo, hi, total))
    return _combine_ln(h, ys, pos_flat, gate.T, g, b, alpha)


def kernel(x, w_qkv, lambda_qk, subln_g, w_o, pool_w, pool_scale, router_w, router_b, moe_w_in, moe_w_out,
           ln_g, ln_b):
    batch, seq, d = x.shape
    depth = ln_g.shape[0]
    alpha = (2.0 * depth) ** 0.25
    tables = _rope_tables(seq)
    router_wt = router_w.T
    outs = []
    for bi in range(batch):
        h = x[bi]
        for i in range(depth):
            j = i // N_MIXERS
            if i % N_MIXERS == 0:
                lam_init = 0.8 - 0.6 * math.exp(-0.3 * i)
                qkv = _qkv_proj(h, w_qkv[j].astype(BF16), tables)
                att = _diff_attention(qkv, lambda_qk[j], subln_g[j], lam_init)
                h = _proj_ln(att, w_o[j].astype(BF16), h, ln_g[i, 0], ln_b[i, 0], alpha)
            else:
                h = _pool_ln(h, pool_w[j].astype(BF16), pool_scale[j], ln_g[i, 0], ln_b[i, 0], alpha)
            h = _moe_ln(h, router_wt, router_b, moe_w_in, moe_w_out, i, ln_g[i, 1], ln_b[i, 1], alpha)
        outs.append(h)
    return jnp.stack(outs)
```

```python
import functools
import math

import jax
import jax.numpy as jnp
from jax import lax
from jax.experimental import pallas as pl
from jax.experimental.pallas import tpu as pltpu

F32 = jnp.float32
BF16 = jnp.bfloat16
I32 = jnp.int32

HEAD_DIM = 128
V_HEAD_DIM = 2 * HEAD_DIM
ROT_DIM = HEAD_DIM // 4
ROT_HALF = ROT_DIM // 2
ROPE_THETA = 500000.0
POOL_WINDOWS = (2, 4, 8, 16)
POOL_HALO = 8
N_EXPERT_GROUPS = 8
EXPERTS_PER_GROUP = 8
TOP_K = 2
LN_EPS = 1e-5
RMS_EPS = 1e-5
N_MIXERS = 2
VMEM_LIMIT_BYTES = 56 * 1024 * 1024

TM_QKV, TN_QKV = 1024, 512
TQ_ATTN, TK_ATTN = 512, 1024
Q_TILES_PER_STEP = 8
SHIFT_MARGIN = 1.001
SAFE_MIN_SUM = 2.0 ** -80
TM_PROJ = 512
TM_POOL = 512
TM_ROUTER = 512
TM_POS = 2048
TM_EXPERT = 256
TM_COMBINE = 512
ROW_UNROLL = 8


def _params(*sem):
    return pltpu.CompilerParams(dimension_semantics=sem, vmem_limit_bytes=VMEM_LIMIT_BYTES)


def _layer_norm(z, g, b):
    mu = jnp.mean(z, axis=-1, keepdims=True)
    zc = z - mu
    var = jnp.mean(zc * zc, axis=-1, keepdims=True)
    return zc * lax.rsqrt(var + LN_EPS) * g + b


def _rope_tables(seq):
    pos = jnp.arange(seq, dtype=F32)
    inv_freq = ROPE_THETA ** (-jnp.arange(0, ROT_DIM, 2, dtype=F32) / ROT_DIM)
    ang = pos[:, None] * inv_freq[None, :]
    cos, sin = jnp.cos(ang), jnp.sin(ang)
    rest = HEAD_DIM - ROT_DIM
    z_half = jnp.zeros((seq, ROT_HALF), F32)
    z_rest = jnp.zeros((seq, rest), F32)
    a = jnp.concatenate([cos, cos, jnp.ones((seq, rest), F32)], axis=1)
    b = jnp.concatenate([z_half, sin, z_rest], axis=1)
    c = jnp.concatenate([-sin, z_half, z_rest], axis=1)
    rot = jnp.stack([a, b, c])
    ident = jnp.stack([jnp.ones((seq, HEAD_DIM), F32), jnp.zeros((seq, HEAD_DIM), F32),
                       jnp.zeros((seq, HEAD_DIM), F32)])
    return jnp.stack([rot * (HEAD_DIM ** -0.5 * math.log2(math.e)), rot, ident])


def _qkv_kernel(x_ref, w_ref, tab_ref, o_ref, xb_ref):
    @pl.when(pl.program_id(1) == 0)
    def _():
        xb_ref[...] = x_ref[...].astype(BF16)

    a, b, c = tab_ref[0], tab_ref[1], tab_ref[2]
    for j in range(o_ref.shape[1] // HEAD_DIM):
        if j % 2 == 0:
            wide = slice(j * HEAD_DIM, (j + 2) * HEAD_DIM)
            y = jnp.dot(xb_ref[...], w_ref[:, wide], preferred_element_type=F32)
        yj = y[:, (j % 2) * HEAD_DIM:(j % 2 + 1) * HEAD_DIM]
        out = yj * a + pltpu.roll(yj, ROT_HALF, 1) * b + pltpu.roll(yj, HEAD_DIM - ROT_HALF, 1) * c
        o_ref[:, j * HEAD_DIM:(j + 1) * HEAD_DIM] = out.astype(o_ref.dtype)


def _qkv_proj(h, w_bf16, tables):
    t, d = h.shape
    n = w_bf16.shape[1]
    tm, tn = min(TM_QKV, t), min(TN_QKV, d)
    return pl.pallas_call(
        _qkv_kernel,
        out_shape=jax.ShapeDtypeStruct((t, n), BF16),
        grid=(t // tm, n // tn),
        in_specs=[
            pl.BlockSpec((tm, d), lambda m, j: (m, 0)),
            pl.BlockSpec((d, tn), lambda m, j: (0, j)),
            pl.BlockSpec((None, 3, tm, HEAD_DIM), lambda m, j: ((j * tn) // d, 0, m, 0)),
        ],
        out_specs=pl.BlockSpec((tm, tn), lambda m, j: (m, j)),
        scratch_shapes=[pltpu.VMEM((tm, d), BF16)],
        compiler_params=_params("parallel", "arbitrary"),
        name="qkv_rope",
    )(h, w_bf16, tables)


def _bound_attn_kernel(lam_ref, g_ref, q_ref, k_ref, v_ref, o_ref,
                       p0_ref, p1_ref, kmax_ref, m_ref, ls_ref, acc_ref, *, tq, tk, lam_init):
    n = k_ref.shape[0] // tk
    n_tiles = q_ref.shape[0] // tq
    p_refs = (p0_ref, p1_ref)
    comp = lambda c: slice(c * HEAD_DIM, (c + 1) * HEAD_DIM)
    nt_dims = (((1,), (1,)), ((), ()))

    lq = lam_ref[...]
    lam = (jnp.exp(jnp.sum(lq[0:1] * lq[1:2], axis=1, keepdims=True))
           - jnp.exp(jnp.sum(lq[2:3] * lq[3:4], axis=1, keepdims=True)) + lam_init)

    @pl.when(pl.program_id(1) == 0)
    def _():
        def chunk(i, carry):
            kk = k_ref[pl.ds(pl.multiple_of(i * tk, tk), tk), :].astype(F32)
            sq = kk * kk
            return tuple(jnp.maximum(carry[c], jnp.max(jnp.sum(sq[:, comp(c)], axis=1, keepdims=True),
                                                       axis=0, keepdims=True)) for c in range(2))

        mx = lax.fori_loop(0, n, chunk, (jnp.zeros((1, 1), F32), jnp.zeros((1, 1), F32)))
        for c in range(2):
            kmax_ref[c] = jnp.broadcast_to(jnp.sqrt(mx[c]), kmax_ref.shape[1:])

    def rows(u):
        return pl.ds(pl.multiple_of(u * tq, tq), tq)

    def setup(st, u):
        for c in range(2):
            qc = q_ref[rows(u), comp(c)].astype(F32)
            qn = jnp.sqrt(jnp.sum(qc * qc, axis=1, keepdims=True))
            bound = qn * kmax_ref[c][0:1, 0:1] * SHIFT_MARGIN
            m_ref[st, c] = jnp.broadcast_to(bound, (tq, HEAD_DIM))
        ls_ref[st] = jnp.zeros(ls_ref.shape[1:], F32)
        acc_ref[st] = jnp.zeros(acc_ref.shape[1:], F32)

    def expo(st, u, j, slot):
        start = pl.multiple_of(j * tk, tk)
        for c in range(2):
            s = lax.dot_general(q_ref[rows(u), comp(c)], k_ref[pl.ds(start, tk), comp(c)], nt_dims,
                                preferred_element_type=F32)
            shift = m_ref[st, c]
            ls = ls_ref[st, c]
            for b in range(tk // HEAD_DIM):
                lanes = slice(b * HEAD_DIM, (b + 1) * HEAD_DIM)
                p = jnp.exp2(s[:, lanes] - shift)
                ls = ls + p
                p_refs[slot][c, :, lanes] = p.astype(BF16)
            ls_ref[st, c] = ls

    def accumulate(st, j, slot):
        start = pl.multiple_of(j * tk, tk)
        vs = v_ref[pl.ds(start, tk), :]
        for c in range(2):
            acc_ref[st, c] = acc_ref[st, c] + jnp.dot(p_refs[slot][c], vs, preferred_element_type=F32)

    def write_out(o, u):
        ms = jnp.mean(o * o, axis=1, keepdims=True)
        of = o * lax.rsqrt(ms + RMS_EPS)
        of = of * g_ref[...] * (1.0 - lam_init)
        o_ref[rows(u), :] = of.astype(o_ref.dtype)

    def safe_tile(u):
        outs = []
        for c in range(2):
            def body(j, carry):
                m_prev, l_prev, acc = carry
                start = pl.multiple_of(j * tk, tk)
                s = lax.dot_general(q_ref[rows(u), comp(c)], k_ref[pl.ds(start, tk), comp(c)], nt_dims,
                                    preferred_element_type=F32)
                m_new = jnp.maximum(m_prev, jnp.max(s, axis=1, keepdims=True))
                alpha = jnp.exp2(m_prev - m_new)
                p = jnp.exp2(s - m_new)
                l_new = alpha * l_prev + jnp.sum(p, axis=1, keepdims=True)
                acc = alpha * acc + jnp.dot(p.astype(BF16), v_ref[pl.ds(start, tk), :],
                                            preferred_element_type=F32)
                return m_new, l_new, acc

            init = (jnp.full((tq, 1), -jnp.inf, F32), jnp.zeros((tq, 1), F32),
                    jnp.zeros((tq, V_HEAD_DIM), F32))
            _, l_fin, acc = lax.fori_loop(0, n, body, init)
            outs.append(acc / l_fin)
        write_out(outs[0] - lam * outs[1], u)

    def finalize(st, u):
        l0 = jnp.sum(ls_ref[st, 0], axis=1, keepdims=True)
        l1 = jnp.sum(ls_ref[st, 1], axis=1, keepdims=True)
        healthy = jnp.min(jnp.minimum(l0, l1)) >= SAFE_MIN_SUM

        @pl.when(healthy)
        def _():
            write_out(acc_ref[st, 0] / l0 - lam * (acc_ref[st, 1] / l1), u)

        @pl.when(jnp.logical_not(healthy))
        def _():
            safe_tile(u)

    if n == 1:
        def single(u, carry):
            setup(0, u)
            expo(0, u, 0, 0)
            accumulate(0, 0, 0)
            finalize(0, u)
            return carry

        lax.fori_loop(0, n_tiles, single, 0)
        return

    setup(0, 0)
    expo(0, 0, 0, 0)

    def tile(u, carry):
        st = u % 2
        nxt = jnp.minimum(u + 1, n_tiles - 1)

        def pair(i, c2):
            j = 2 * i
            expo(st, u, j + 1, 1)
            accumulate(st, j, 0)
            expo(st, u, j + 2, 0)
            accumulate(st, j + 1, 1)
            return c2

        lax.fori_loop(0, n // 2 - 1, pair, 0)
        expo(st, u, n - 1, 1)
        accumulate(st, n - 2, 0)
        setup(1 - st, nxt)
        expo(1 - st, nxt, 0, 0)
        accumulate(st, n - 1, 1)
        finalize(st, u)
        return carry

    lax.fori_loop(0, n_tiles, tile, 0)


def _diff_attention(qkv, lam_qk, subln_g, lam_init):
    t = qkv.shape[0]
    d = qkv.shape[1] // 3
    n_heads = d // V_HEAD_DIM
    tq, tk = min(TQ_ATTN, t), min(TK_ATTN, t)
    tq_step = min(tq * Q_TILES_PER_STEP, t)
    assert t // tk == 1 or (t // tk) % 2 == 0
    kern = functools.partial(_bound_attn_kernel, tq=tq, tk=tk, lam_init=lam_init)
    state = pltpu.VMEM((2, 2, tq, HEAD_DIM), F32)
    return pl.pallas_call(
        kern,
        out_shape=jax.ShapeDtypeStruct((t, d), BF16),
        grid=(n_heads, t // tq_step),
        in_specs=[
            pl.BlockSpec((4, HEAD_DIM), lambda h, i: (0, 0)),
            pl.BlockSpec((1, V_HEAD_DIM), lambda h, i: (0, 0)),
            pl.BlockSpec((tq_step, V_HEAD_DIM), lambda h, i: (i, h)),
            pl.BlockSpec((t, V_HEAD_DIM), lambda h, i: (0, n_heads + h), pipeline_mode=pl.Buffered(1)),
            pl.BlockSpec((t, V_HEAD_DIM), lambda h, i: (0, 2 * n_heads + h), pipeline_mode=pl.Buffered(1)),
        ],
        out_specs=pl.BlockSpec((tq_step, V_HEAD_DIM), lambda h, i: (i, h)),
        scratch_shapes=[
            pltpu.VMEM((2, tq, tk), BF16), pltpu.VMEM((2, tq, tk), BF16),
            pltpu.VMEM((2, 8, HEAD_DIM), F32),
            state, state,
            pltpu.VMEM((2, 2, tq, V_HEAD_DIM), F32),
        ],
        compiler_params=_params("arbitrary", "arbitrary"),
        name="diff_attn",
    )(lam_qk, subln_g.reshape(1, V_HEAD_DIM), qkv, qkv, qkv)


def _proj_ln_kernel(a_ref, w_ref, h_ref, g_ref, b_ref, o_ref, *, alpha):
    mix = jnp.dot(a_ref[...], w_ref[...], preferred_element_type=F32)
    z = alpha * h_ref[...] + mix
    o_ref[...] = _layer_norm(z, g_ref[...], b_ref[...])


def _proj_ln(a, w_bf16, h, g, b, alpha):
    t, d = h.shape
    tm = min(TM_PROJ, t)
    return pl.pallas_call(
        functools.partial(_proj_ln_kernel, alpha=alpha),
        out_shape=jax.ShapeDtypeStruct((t, d), F32),
        grid=(t // tm,),
        in_specs=[
            pl.BlockSpec((tm, d), lambda i: (i, 0)),
            pl.BlockSpec((d, d), lambda i: (0, 0), pipeline_mode=pl.Buffered(1)),
            pl.BlockSpec((tm, d), lambda i: (i, 0)),
            pl.BlockSpec((1, d), lambda i: (0, 0)),
            pl.BlockSpec((1, d), lambda i: (0, 0)),
        ],
        out_specs=pl.BlockSpec((tm, d), lambda i: (i, 0)),
        compiler_params=_params("parallel"),
        name="proj_ln",
    )(a, w_bf16, h, g.reshape(1, d), b.reshape(1, d))


def _pool_ln_kernel(h_ref, hp_ref, hn_ref, w_ref, sc_ref, g_ref, b_ref, o_ref, ext_ref, z_ref, *, alpha, seq):
    i = pl.program_id(0)
    tm, d = h_ref.shape
    c = d // len(POOL_WINDOWS)
    ext_ref[0:POOL_HALO, :] = jnp.where(i > 0, hp_ref[...], 0.0)
    ext_ref[POOL_HALO:POOL_HALO + tm, :] = h_ref[...]
    ext_ref[POOL_HALO + tm:2 * POOL_HALO + tm, :] = jnp.where(i < pl.num_programs(0) - 1, hn_ref[...], 0.0)
    t = i * tm + lax.broadcasted_iota(I32, (tm, 1), 0)
    for g, w in enumerate(POOL_WINDOWS):
        half = w // 2
        cols = slice(g * c, (g + 1) * c)
        win = ext_ref[POOL_HALO - half:POOL_HALO - half + tm, cols]
        for j in range(1 - half, half):
            win = win + ext_ref[POOL_HALO + j:POOL_HALO + j + tm, cols]
        cnt = (jnp.minimum(t + half, seq) - jnp.maximum(t - half, 0)).astype(F32)
        x = h_ref[:, cols]
        diff = win / cnt - x
        y = jnp.dot(diff.astype(BF16), w_ref[g], preferred_element_type=F32) * sc_ref[:, cols]
        z_ref[:, cols] = alpha * x + y
    o_ref[...] = _layer_norm(z_ref[...], g_ref[...], b_ref[...])


def _pool_ln(h, w_bf16, scale, g, b, alpha):
    t, d = h.shape
    tm = min(TM_POOL, t)
    hb = tm // POOL_HALO
    n_halo = t // POOL_HALO
    c = d // len(POOL_WINDOWS)
    return pl.pallas_call(
        functools.partial(_pool_ln_kernel, alpha=alpha, seq=t),
        out_shape=jax.ShapeDtypeStruct((t, d), F32),
        grid=(t // tm,),
        in_specs=[
            pl.BlockSpec((tm, d), lambda i: (i, 0)),
            pl.BlockSpec((POOL_HALO, d), lambda i: (jnp.maximum(i * hb - 1, 0), 0)),
            pl.BlockSpec((POOL_HALO, d), lambda i: (jnp.minimum((i + 1) * hb, n_halo - 1), 0)),
            pl.BlockSpec((len(POOL_WINDOWS), c, c), lambda i: (0, 0, 0)),
            pl.BlockSpec((1, d), lambda i: (0, 0)),
            pl.BlockSpec((1, d), lambda i: (0, 0)),
            pl.BlockSpec((1, d), lambda i: (0, 0)),
        ],
        out_specs=pl.BlockSpec((tm, d), lambda i: (i, 0)),
        scratch_shapes=[pltpu.VMEM((tm + 2 * POOL_HALO, d), F32), pltpu.VMEM((tm, d), F32)],
        compiler_params=_params("parallel"),
        name="pool_ln",
    )(h, h, h, w_bf16, scale.reshape(1, d), g.reshape(1, d), b.reshape(1, d))


def _split_bf16(x):
    hi = x.astype(BF16)
    lo = (x - hi.astype(F32)).astype(BF16)
    return hi, lo


def _router_kernel(h_ref, wt_ref, b_ref, idx_ref, gate_ref, rank_ref, cnt_ref, carry_ref, tri_ref):
    tm = h_ref.shape[0]
    n_exp = wt_ref.shape[0]
    n_grp, per = N_EXPERT_GROUPS, EXPERTS_PER_GROUP

    @pl.when(pl.program_id(0) == 0)
    def _():
        carry_ref[...] = jnp.zeros(carry_ref.shape, F32)
        r = lax.broadcasted_iota(I32, (tm, tm), 0)
        c = lax.broadcasted_iota(I32, (tm, tm), 1)
        tri_ref[...] = jnp.where(r <= c, 1.0, 0.0).astype(BF16)

    h_hi, h_lo = _split_bf16(h_ref[...])
    w_hi, w_lo = _split_bf16(wt_ref[...])
    dn = (((1,), (1,)), ((), ()))
    logits = (lax.dot_general(w_hi, h_hi, dn, preferred_element_type=F32)
              + lax.dot_general(w_hi, h_lo, dn, preferred_element_type=F32)
              + lax.dot_general(w_lo, h_hi, dn, preferred_element_type=F32))
    scores = 1.0 / (1.0 + jnp.exp(-logits))
    biased = scores + b_ref[...]

    v = biased.reshape(n_grp, per, tm)
    piota = lax.broadcasted_iota(I32, (n_grp, per, tm), 1)
    m1 = jnp.max(v, axis=1, keepdims=True)
    i1 = jnp.min(jnp.where(v == m1, piota, per), axis=1, keepdims=True)
    v2 = jnp.where(piota == i1, -jnp.inf, v)
    m2 = jnp.max(v2, axis=1, keepdims=True)
    i2 = jnp.min(jnp.where(v2 == m2, piota, per), axis=1, keepdims=True)
    grp_score = m1 + m2
    giota = lax.broadcasted_iota(I32, (n_grp, 1, tm), 0)
    best = jnp.min(jnp.where(grp_score == jnp.max(grp_score, axis=0, keepdims=True), giota, n_grp),
                   axis=0, keepdims=True)
    sel = giota == best
    e0 = jnp.sum(jnp.where(sel, giota * per + i1, 0), axis=0)
    e1 = jnp.sum(jnp.where(sel, giota * per + i2, 0), axis=0)

    eiota = lax.broadcasted_iota(I32, (n_exp, tm), 0)
    oh0 = eiota == e0
    oh1 = eiota == e1
    s0 = jnp.sum(jnp.where(oh0, scores, 0.0), axis=0, keepdims=True)
    s1 = jnp.sum(jnp.where(oh1, scores, 0.0), axis=0, keepdims=True)
    den = s0 + s1

    member = jnp.where(oh0, 1.0, jnp.where(oh1, 1.0, 0.0)).astype(BF16)
    cnt = jnp.dot(member, tri_ref[...], preferred_element_type=F32)
    tot = cnt + carry_ref[:, 0:1]
    r0 = jnp.sum(jnp.where(oh0, tot, 0.0), axis=0, keepdims=True) - 1.0
    r1 = jnp.sum(jnp.where(oh1, tot, 0.0), axis=0, keepdims=True) - 1.0
    carry_ref[...] = carry_ref[...] + cnt[:, tm - 1:tm]

    idx_ref[0:1, :] = e0
    idx_ref[1:2, :] = e1
    gate_ref[0:1, :] = s0 / den
    gate_ref[1:2, :] = s1 / den
    rank_ref[0:1, :] = r0.astype(I32)
    rank_ref[1:2, :] = r1.astype(I32)
    cnt_ref[...] = carry_ref[...]


def _router(h, router_wt, router_b):
    t, d = h.shape
    n_exp = router_wt.shape[0]
    tm = min(TM_ROUTER, t)
    pair = lambda dt: jax.ShapeDtypeStruct((TOP_K, t), dt)
    pair_spec = pl.BlockSpec((TOP_K, tm), lambda i: (0, i))
    return pl.pallas_call(
        _router_kernel,
        out_shape=(pair(I32), pair(F32), pair(I32), jax.ShapeDtypeStruct((n_exp, HEAD_DIM), F32)),
        grid=(t // tm,),
        in_specs=[
            pl.BlockSpec((tm, d), lambda i: (i, 0)),
            pl.BlockSpec((n_exp, d), lambda i: (0, 0)),
            pl.BlockSpec((n_exp, 1), lambda i: (0, 0)),
        ],
        out_specs=(pair_spec, pair_spec, pair_spec, pl.BlockSpec((n_exp, HEAD_DIM), lambda i: (0, 0))),
        scratch_shapes=[pltpu.VMEM((n_exp, HEAD_DIM), F32), pltpu.VMEM((tm, tm), BF16)],
        compiler_params=_params("arbitrary"),
        name="router",
    )(h, router_wt, router_b.reshape(n_exp, 1))


def _pos_kernel(idx_ref, rank_ref, starts_ref, pos_ref):
    n_exp = starts_ref.shape[0]
    tm = idx_ref.shape[1]
    eiota = lax.broadcasted_iota(I32, (n_exp, tm), 0)
    st = starts_ref[...]
    for k in range(TOP_K):
        oh = eiota == idx_ref[k:k + 1, :]
        pos_ref[k:k + 1, :] = jnp.sum(jnp.where(oh, st, 0), axis=0, keepdims=True) + rank_ref[k:k + 1, :]


def _positions(idx, rank, starts):
    t = idx.shape[1]
    n_exp = starts.shape[0]
    tm = min(TM_POS, t)
    pair_spec = pl.BlockSpec((TOP_K, tm), lambda i: (0, i))
    return pl.pallas_call(
        _pos_kernel,
        out_shape=jax.ShapeDtypeStruct((TOP_K, t), I32),
        grid=(t // tm,),
        in_specs=[pair_spec, pair_spec, pl.BlockSpec((n_exp, 1), lambda i: (0, 0))],
        out_specs=pair_spec,
        compiler_params=_params("parallel"),
        name="positions",
    )(idx, rank, starts.reshape(n_exp, 1))


def _invert_kernel(pos_ref, inv_ref):
    def body(i, carry):
        inv_ref[pos_ref[i]] = i
        return carry

    lax.fori_loop(0, pos_ref.shape[0], body, 0, unroll=8)


def _invert(pos_flat):
    return pl.pallas_call(
        _invert_kernel,
        out_shape=jax.ShapeDtypeStruct(pos_flat.shape, I32),
        grid_spec=pltpu.PrefetchScalarGridSpec(
            num_scalar_prefetch=1,
            grid=(1,),
            in_specs=[],
            out_specs=pl.BlockSpec(memory_space=pltpu.SMEM),
        ),
        compiler_params=_params("arbitrary"),
        name="invert",
    )(pos_flat)


def _expert_kernel(tile_ref, exp_ref, lo_ref, hi_ref, nw_ref, inv_ref, h_ref, win_ref, wout_ref, y_ref,
                   xbuf_ref, ybuf_ref, winb_ref, woutb_ref, gsem, ssem):
    w = pl.program_id(0)
    n_work = nw_ref[0]
    _, tm, d = xbuf_ref.shape
    n_tok = h_ref.shape[0]
    d_ff = wout_ref.shape[0]
    prev = jnp.maximum(w - 1, 0)
    new_expert = jnp.logical_or(w == 0, exp_ref[w] != exp_ref[prev])
    active = w < n_work

    def token_of(pair):
        if n_tok & (n_tok - 1) == 0:
            return pair & (n_tok - 1)
        return pair - (pair >= n_tok).astype(I32) * n_tok

    def gather_copy(item, row0, r):
        buf = item % 2
        t = token_of(inv_ref[row0 + r])
        return pltpu.make_async_copy(h_ref.at[pl.ds(t, 1), :], xbuf_ref.at[buf, pl.ds(r, 1), :], gsem.at[buf])

    def scatter_copy(item, row0, r):
        buf = item % 2
        return pltpu.make_async_copy(ybuf_ref.at[buf, pl.ds(r, 1), :], y_ref.at[pl.ds(inv_ref[row0 + r], 1), :],
                                     ssem.at[buf])

    def for_rows(item, fn):
        lo, hi = lo_ref[item], hi_ref[item]
        row0 = tile_ref[item] * tm
        n_groups = (hi - lo) // ROW_UNROLL

        def group(i, carry):
            for u in range(ROW_UNROLL):
                fn(item, row0, lo + i * ROW_UNROLL + u)
            return carry

        def single(r, carry):
            fn(item, row0, r)
            return carry

        lax.fori_loop(0, n_groups, group, 0)
        lax.fori_loop(lo + n_groups * ROW_UNROLL, hi, single, 0)

    start_gather = lambda item, row0, r: gather_copy(item, row0, r).start()
    wait_gather = lambda item, row0, r: gather_copy(item, row0, r).wait()
    start_scatter = lambda item, row0, r: scatter_copy(item, row0, r).start()
    wait_scatter = lambda item, row0, r: scatter_copy(item, row0, r).wait()

    @pl.when(w == 0)
    def _():
        xbuf_ref[...] = jnp.zeros(xbuf_ref.shape, F32)
        for_rows(0, start_gather)

    @pl.when(w + 1 < n_work)
    def _():
        for_rows(w + 1, start_gather)

    @pl.when(jnp.logical_and(active, new_expert))
    def _():
        winb_ref[...] = win_ref[...].astype(BF16)
        woutb_ref[...] = wout_ref[...].astype(BF16)

    @pl.when(active)
    def _():
        buf = w % 2
        for_rows(w, wait_gather)
        hid = jnp.dot(xbuf_ref[buf].astype(BF16), winb_ref[...], preferred_element_type=F32)
        hg, hu = hid[:, :d_ff], hid[:, d_ff:]
        act = hg * (1.0 / (1.0 + jnp.exp(-hg))) * hu
        ybuf_ref[buf] = jnp.dot(act.astype(BF16), woutb_ref[...], preferred_element_type=F32)
        for_rows(w, start_scatter)

        @pl.when(w >= 1)
        def _():
            for_rows(w - 1, wait_scatter)

        @pl.when(w == n_work - 1)
        def _():
            for_rows(w, wait_scatter)


def _work_items(counts, tm, n_rows):
    n_exp = counts.shape[0]
    n_tiles = n_rows // tm
    n_work = n_tiles + n_exp - 1
    ends = jnp.cumsum(counts)
    starts = ends - counts
    first = starts // tm
    last = jnp.maximum(ends - 1, starts) // tm
    n_items = jnp.where(counts > 0, last - first + 1, 0)
    item_end = jnp.cumsum(n_items)
    item_start = item_end - n_items
    total = item_end[-1]
    w = jnp.arange(n_work, dtype=I32)
    wc = jnp.minimum(w, total - 1)
    e = jnp.searchsorted(item_end, wc, side="right").astype(I32)
    tile = first[e] + (wc - item_start[e])
    lo = jnp.maximum(starts[e], tile * tm) - tile * tm
    hi = jnp.minimum(ends[e], (tile + 1) * tm) - tile * tm
    return tile.astype(I32), e, lo.astype(I32), hi.astype(I32), total.reshape(1).astype(I32), starts.astype(I32)


def _experts(h, w_in, w_out, layer, items, inv):
    t, d = h.shape
    two_f = w_in.shape[3]
    d_ff = w_out.shape[2]
    tm = min(TM_EXPERT, TOP_K * t)
    tile, e, lo, hi, total = items
    n_work = tile.shape[0]
    weights = lambda w, tl, ex, lo, hi, n, inv: (layer, ex[w], 0, 0)
    return pl.pallas_call(
        _expert_kernel,
        out_shape=jax.ShapeDtypeStruct((TOP_K * t, d), F32),
        grid_spec=pltpu.PrefetchScalarGridSpec(
            num_scalar_prefetch=6,
            grid=(n_work,),
            in_specs=[
                pl.BlockSpec(memory_space=pl.ANY),
                pl.BlockSpec((None, None, d, two_f), weights),
                pl.BlockSpec((None, None, d_ff, d), weights),
            ],
            out_specs=pl.BlockSpec(memory_space=pl.ANY),
            scratch_shapes=[
                pltpu.VMEM((2, tm, d), F32), pltpu.VMEM((2, tm, d), F32),
                pltpu.VMEM((d, two_f), BF16), pltpu.VMEM((d_ff, d), BF16),
                pltpu.SemaphoreType.DMA((2,)), pltpu.SemaphoreType.DMA((2,)),
            ],
        ),
        compiler_params=_params("arbitrary"),
        name="experts",
    )(tile, e, lo, hi, total, inv, h, w_in, w_out)


def _combine_kernel(h_ref, y0_ref, y1_ref, gate_ref, g_ref, b_ref, o_ref, *, alpha):
    gate = gate_ref[...]
    ffn = y0_ref[...] * gate[:, 0:1] + y1_ref[...] * gate[:, 1:2]
    z = alpha * h_ref[...] + ffn
    o_ref[...] = _layer_norm(z, g_ref[...], b_ref[...])


def _combine_ln(h, y, gate_tk, g, b, alpha):
    t, d = h.shape
    tm = min(TM_COMBINE, t)
    return pl.pallas_call(
        functools.partial(_combine_kernel, alpha=alpha),
        out_shape=jax.ShapeDtypeStruct((t, d), F32),
        grid=(t // tm,),
        in_specs=[
            pl.BlockSpec((tm, d), lambda i: (i, 0)),
            pl.BlockSpec((tm, d), lambda i: (i, 0)),
            pl.BlockSpec((tm, d), lambda i: (t // tm + i, 0)),
            pl.BlockSpec((tm, TOP_K), lambda i: (i, 0)),
            pl.BlockSpec((1, d), lambda i: (0, 0)),
            pl.BlockSpec((1, d), lambda i: (0, 0)),
        ],
        out_specs=pl.BlockSpec((tm, d), lambda i: (i, 0)),
        compiler_params=_params("parallel"),
        name="combine_ln",
    )(h, y, y, gate_tk, g.reshape(1, d), b.reshape(1, d))


def _moe_ln(h, router_wt, router_b, w_in, w_out, layer, g, b, alpha):
    t = h.shape[0]
    idx, gate, rank, cnt = _router(h, router_wt, router_b)
    counts = cnt[:, 0].astype(I32)
    tile, e, lo, hi, total, starts = _work_items(counts, min(TM_EXPERT, TOP_K * t), TOP_K * t)
    pos_flat = _positions(idx, rank, starts).reshape(TOP_K * t)
    y = _experts(h, w_in, w_out, layer, (tile, e, lo, hi, total), _invert(pos_flat))
    return _combine_ln(h, y, gate.T, g, b, alpha)


def kernel(x, w_qkv, lambda_qk, subln_g, w_o, pool_w, pool_scale, router_w, router_b, moe_w_in, moe_w_out,
           ln_g, ln_b):
    batch, seq, d = x.shape
    depth = ln_g.shape[0]
    alpha = (2.0 * depth) ** 0.25
    tables = _rope_tables(seq)
    router_wt = router_w.T
    outs = []
    for bi in range(batch):
        h = x[bi]
        for i in range(depth):
            j = i // N_MIXERS
            if i % N_MIXERS == 0:
                lam_init = 0.8 - 0.6 * math.exp(-0.3 * i)
                qkv = _qkv_proj(h, w_qkv[j].astype(BF16), tables)
                att = _diff_attention(qkv, lambda_qk[j], subln_g[j], lam_init)
                h = _proj_ln(att, w_o[j].astype(BF16), h, ln_g[i, 0], ln_b[i, 0], alpha)
            else:
                h = _pool_ln(h, pool_w[j].astype(BF16), pool_scale[j], ln_g[i, 0], ln_b[i, 0], alpha)
            h = _moe_ln(h, router_wt, router_b, moe_w_in, moe_w_out, i, ln_g[i, 1], ln_b[i, 1], alpha)
        outs.append(h)
    return jnp.stack(outs)
```

```python
import functools
import math

import jax
import jax.numpy as jnp
from jax import lax
from jax.experimental import pallas as pl
from jax.experimental.pallas import tpu as pltpu

F32 = jnp.float32
BF16 = jnp.bfloat16
I32 = jnp.int32

HEAD_DIM = 128
V_HEAD_DIM = 2 * HEAD_DIM
ROT_DIM = HEAD_DIM // 4
ROT_HALF = ROT_DIM // 2
ROPE_THETA = 500000.0
POOL_WINDOWS = (2, 4, 8, 16)
POOL_HALO = 8
N_EXPERT_GROUPS = 8
EXPERTS_PER_GROUP = 8
TOP_K = 2
LN_EPS = 1e-5
RMS_EPS = 1e-5
N_MIXERS = 2
VMEM_LIMIT_BYTES = 56 * 1024 * 1024

TM_QKV, TN_QKV = 1024, 512
TQ_ATTN, TK_ATTN = 512, 1024
Q_TILES_PER_STEP = 8
SHIFT_MARGIN = 1.001
SAFE_MIN_SUM = 2.0 ** -80
TM_PROJ = 512
TM_POOL = 512
TM_ROUTER = 512
TM_POS = 2048
TM_EXPERT = 256
TM_COMBINE = 512
ROW_UNROLL = 8


def _params(*sem):
    return pltpu.CompilerParams(dimension_semantics=sem, vmem_limit_bytes=VMEM_LIMIT_BYTES)


def _layer_norm(z, g, b):
    mu = jnp.mean(z, axis=-1, keepdims=True)
    zc = z - mu
    var = jnp.mean(zc * zc, axis=-1, keepdims=True)
    return zc * lax.rsqrt(var + LN_EPS) * g + b


def _rope_tables(seq):
    pos = jnp.arange(seq, dtype=F32)
    inv_freq = ROPE_THETA ** (-jnp.arange(0, ROT_DIM, 2, dtype=F32) / ROT_DIM)
    ang = pos[:, None] * inv_freq[None, :]
    cos, sin = jnp.cos(ang), jnp.sin(ang)
    rest = HEAD_DIM - ROT_DIM
    z_half = jnp.zeros((seq, ROT_HALF), F32)
    z_rest = jnp.zeros((seq, rest), F32)
    a = jnp.concatenate([cos, cos, jnp.ones((seq, rest), F32)], axis=1)
    b = jnp.concatenate([z_half, sin, z_rest], axis=1)
    c = jnp.concatenate([-sin, z_half, z_rest], axis=1)
    rot = jnp.stack([a, b, c])
    ident = jnp.stack([jnp.ones((seq, HEAD_DIM), F32), jnp.zeros((seq, HEAD_DIM), F32),
                       jnp.zeros((seq, HEAD_DIM), F32)])
    return jnp.stack([rot * (HEAD_DIM ** -0.5 * math.log2(math.e)), ident]), cos.T, sin.T


def _qkv_kernel(x_ref, w_ref, tab_ref, o_ref, xb_ref):
    @pl.when(pl.program_id(1) == 0)
    def _():
        xb_ref[...] = x_ref[...].astype(BF16)

    a, b, c = tab_ref[0], tab_ref[1], tab_ref[2]
    for j in range(o_ref.shape[1] // HEAD_DIM):
        if j % 2 == 0:
            wide = slice(j * HEAD_DIM, (j + 2) * HEAD_DIM)
            y = jnp.dot(xb_ref[...], w_ref[:, wide], preferred_element_type=F32)
        yj = y[:, (j % 2) * HEAD_DIM:(j % 2 + 1) * HEAD_DIM]
        out = yj * a + pltpu.roll(yj, ROT_HALF, 1) * b + pltpu.roll(yj, HEAD_DIM - ROT_HALF, 1) * c
        o_ref[:, j * HEAD_DIM:(j + 1) * HEAD_DIM] = out.astype(o_ref.dtype)


def _qkv_proj(h, w_bf16, tables):
    t, d = h.shape
    n = w_bf16.shape[1]
    tm, tn = min(TM_QKV, t), min(TN_QKV, d)
    return pl.pallas_call(
        _qkv_kernel,
        out_shape=jax.ShapeDtypeStruct((t, n), BF16),
        grid=(t // tm, n // tn),
        in_specs=[
            pl.BlockSpec((tm, d), lambda m, j: (m, 0)),
            pl.BlockSpec((d, tn), lambda m, j: (0, j)),
            pl.BlockSpec((None, 3, tm, HEAD_DIM), lambda m, j: ((j * tn) // d, 0, m, 0)),
        ],
        out_specs=pl.BlockSpec((tm, tn), lambda m, j: (m, j)),
        scratch_shapes=[pltpu.VMEM((tm, d), BF16)],
        compiler_params=_params("parallel", "arbitrary"),
        name="qkv_rope",
    )(h, w_bf16, tables)


def _kt_kernel(w_ref, x_ref, cos_ref, sin_ref, o_ref, xb_ref):
    @pl.when(pl.program_id(1) == 0)
    def _():
        xb_ref[...] = x_ref[...].astype(BF16)

    y = lax.dot_general(w_ref[...], xb_ref[...], (((1,), (1,)), ((), ())), preferred_element_type=F32)
    cos, sin = cos_ref[...], sin_ref[...]
    for g in range(y.shape[0] // HEAD_DIM):
        r0 = g * HEAD_DIM
        t1 = y[r0:r0 + ROT_HALF]
        t2 = y[r0 + ROT_HALF:r0 + ROT_DIM]
        o_ref[r0:r0 + ROT_HALF, :] = (t1 * cos - t2 * sin).astype(o_ref.dtype)
        o_ref[r0 + ROT_HALF:r0 + ROT_DIM, :] = (t2 * cos + t1 * sin).astype(o_ref.dtype)
        o_ref[r0 + ROT_DIM:r0 + HEAD_DIM, :] = y[r0 + ROT_DIM:r0 + HEAD_DIM].astype(o_ref.dtype)


def _k_proj_t(h, wt_bf16, cos_t, sin_t):
    t, d = h.shape
    n = wt_bf16.shape[0]
    tm, tn = min(TM_QKV, t), min(TN_QKV, n)
    return pl.pallas_call(
        _kt_kernel,
        out_shape=jax.ShapeDtypeStruct((n, t), BF16),
        grid=(t // tm, n // tn),
        in_specs=[
            pl.BlockSpec((tn, d), lambda m, j: (j, 0)),
            pl.BlockSpec((tm, d), lambda m, j: (m, 0)),
            pl.BlockSpec((ROT_HALF, tm), lambda m, j: (0, m)),
            pl.BlockSpec((ROT_HALF, tm), lambda m, j: (0, m)),
        ],
        out_specs=pl.BlockSpec((tn, tm), lambda m, j: (j, m)),
        scratch_shapes=[pltpu.VMEM((tm, d), BF16)],
        compiler_params=_params("parallel", "arbitrary"),
        name="k_rope_t",
    )(wt_bf16, h, cos_t, sin_t)


def _bound_attn_kernel(lam_ref, g_ref, q_ref, kt_ref, v_ref, o_ref,
                       p0_ref, p1_ref, kmax_ref, m_ref, ls_ref, acc_ref, *, tq, tk, lam_init):
    n = kt_ref.shape[1] // tk
    n_tiles = q_ref.shape[0] // tq
    p_refs = (p0_ref, p1_ref)
    comp = lambda c: slice(c * HEAD_DIM, (c + 1) * HEAD_DIM)

    def scores(u, c, start):
        return jnp.dot(q_ref[rows(u), comp(c)], kt_ref[comp(c), pl.ds(start, tk)], preferred_element_type=F32)

    lq = lam_ref[...]
    lam = (jnp.exp(jnp.sum(lq[0:1] * lq[1:2], axis=1, keepdims=True))
           - jnp.exp(jnp.sum(lq[2:3] * lq[3:4], axis=1, keepdims=True)) + lam_init)

    @pl.when(pl.program_id(1) == 0)
    def _():
        def chunk(i, carry):
            kk = kt_ref[:, pl.ds(pl.multiple_of(i * tk, tk), tk)].astype(F32)
            sq = kk * kk
            return tuple(jnp.maximum(carry[c], jnp.max(jnp.sum(sq[comp(c), :], axis=0, keepdims=True),
                                                       axis=1, keepdims=True)) for c in range(2))

        mx = lax.fori_loop(0, n, chunk, (jnp.zeros((1, 1), F32), jnp.zeros((1, 1), F32)))
        for c in range(2):
            kmax_ref[c] = jnp.broadcast_to(jnp.sqrt(mx[c]), kmax_ref.shape[1:])

    def rows(u):
        return pl.ds(pl.multiple_of(u * tq, tq), tq)

    def setup(st, u):
        for c in range(2):
            qc = q_ref[rows(u), comp(c)].astype(F32)
            qn = jnp.sqrt(jnp.sum(qc * qc, axis=1, keepdims=True))
            bound = qn * kmax_ref[c][0:1, 0:1] * SHIFT_MARGIN
            m_ref[st, c] = jnp.broadcast_to(bound, (tq, HEAD_DIM))
        ls_ref[st] = jnp.zeros(ls_ref.shape[1:], F32)
        acc_ref[st] = jnp.zeros(acc_ref.shape[1:], F32)

    def expo(st, u, j, slot):
        start = pl.multiple_of(j * tk, tk)
        for c in range(2):
            s = scores(u, c, start)
            shift = m_ref[st, c]
            ls = ls_ref[st, c]
            for b in range(tk // HEAD_DIM):
                lanes = slice(b * HEAD_DIM, (b + 1) * HEAD_DIM)
                p = jnp.exp2(s[:, lanes] - shift)
                ls = ls + p
                p_refs[slot][c, :, lanes] = p.astype(BF16)
            ls_ref[st, c] = ls

    def accumulate(st, j, slot):
        start = pl.multiple_of(j * tk, tk)
        vs = v_ref[pl.ds(start, tk), :]
        for c in range(2):
            acc_ref[st, c] = acc_ref[st, c] + jnp.dot(p_refs[slot][c], vs, preferred_element_type=F32)

    def write_out(o, u):
        ms = jnp.mean(o * o, axis=1, keepdims=True)
        of = o * lax.rsqrt(ms + RMS_EPS)
        of = of * g_ref[...] * (1.0 - lam_init)
        o_ref[rows(u), :] = of.astype(o_ref.dtype)

    def safe_tile(u):
        outs = []
        for c in range(2):
            def body(j, carry):
                m_prev, l_prev, acc = carry
                start = pl.multiple_of(j * tk, tk)
                s = scores(u, c, start)
                m_new = jnp.maximum(m_prev, jnp.max(s, axis=1, keepdims=True))
                alpha = jnp.exp2(m_prev - m_new)
                p = jnp.exp2(s - m_new)
                l_new = alpha * l_prev + jnp.sum(p, axis=1, keepdims=True)
                acc = alpha * acc + jnp.dot(p.astype(BF16), v_ref[pl.ds(start, tk), :],
                                            preferred_element_type=F32)
                return m_new, l_new, acc

            init = (jnp.full((tq, 1), -jnp.inf, F32), jnp.zeros((tq, 1), F32),
                    jnp.zeros((tq, V_HEAD_DIM), F32))
            _, l_fin, acc = lax.fori_loop(0, n, body, init)
            outs.append(acc / l_fin)
        write_out(outs[0] - lam * outs[1], u)

    def finalize(st, u):
        l0 = jnp.sum(ls_ref[st, 0], axis=1, keepdims=True)
        l1 = jnp.sum(ls_ref[st, 1], axis=1, keepdims=True)
        healthy = jnp.min(jnp.minimum(l0, l1)) >= SAFE_MIN_SUM

        @pl.when(healthy)
        def _():
            write_out(acc_ref[st, 0] / l0 - lam * (acc_ref[st, 1] / l1), u)

        @pl.when(jnp.logical_not(healthy))
        def _():
            safe_tile(u)

    if n == 1:
        def single(u, carry):
            setup(0, u)
            expo(0, u, 0, 0)
            accumulate(0, 0, 0)
            finalize(0, u)
            return carry

        lax.fori_loop(0, n_tiles, single, 0)
        return

    setup(0, 0)
    expo(0, 0, 0, 0)

    def tile(u, carry):
        st = u % 2
        nxt = jnp.minimum(u + 1, n_tiles - 1)

        def pair(i, c2):
            j = 2 * i
            expo(st, u, j + 1, 1)
            accumulate(st, j, 0)
            expo(st, u, j + 2, 0)
            accumulate(st, j + 1, 1)
            return c2

        lax.fori_loop(0, n // 2 - 1, pair, 0)
        expo(st, u, n - 1, 1)
        accumulate(st, n - 2, 0)
        setup(1 - st, nxt)
        expo(1 - st, nxt, 0, 0)
        accumulate(st, n - 1, 1)
        finalize(st, u)
        return carry

    lax.fori_loop(0, n_tiles, tile, 0)


def _diff_attention(qv, kt, lam_qk, subln_g, lam_init):
    t = qv.shape[0]
    d = qv.shape[1] // 2
    n_heads = d // V_HEAD_DIM
    tq, tk = min(TQ_ATTN, t), min(TK_ATTN, t)
    tq_step = min(tq * Q_TILES_PER_STEP, t)
    assert t // tk == 1 or (t // tk) % 2 == 0
    kern = functools.partial(_bound_attn_kernel, tq=tq, tk=tk, lam_init=lam_init)
    state = pltpu.VMEM((2, 2, tq, HEAD_DIM), F32)
    return pl.pallas_call(
        kern,
        out_shape=jax.ShapeDtypeStruct((t, d), BF16),
        grid=(n_heads, t // tq_step),
        in_specs=[
            pl.BlockSpec((4, HEAD_DIM), lambda h, i: (0, 0)),
            pl.BlockSpec((1, V_HEAD_DIM), lambda h, i: (0, 0)),
            pl.BlockSpec((tq_step, V_HEAD_DIM), lambda h, i: (i, h)),
            pl.BlockSpec((V_HEAD_DIM, t), lambda h, i: (h, 0), pipeline_mode=pl.Buffered(1)),
            pl.BlockSpec((t, V_HEAD_DIM), lambda h, i: (0, n_heads + h), pipeline_mode=pl.Buffered(1)),
        ],
        out_specs=pl.BlockSpec((tq_step, V_HEAD_DIM), lambda h, i: (i, h)),
        scratch_shapes=[
            pltpu.VMEM((2, tq, tk), BF16), pltpu.VMEM((2, tq, tk), BF16),
            pltpu.VMEM((2, 8, HEAD_DIM), F32),
            state, state,
            pltpu.VMEM((2, 2, tq, V_HEAD_DIM), F32),
        ],
        compiler_params=_params("arbitrary", "arbitrary"),
        name="diff_attn",
    )(lam_qk, subln_g.reshape(1, V_HEAD_DIM), qv, kt, qv)


def _proj_ln_kernel(a_ref, w_ref, h_ref, g_ref, b_ref, o_ref, *, alpha):
    mix = jnp.dot(a_ref[...], w_ref[...], preferred_element_type=F32)
    z = alpha * h_ref[...] + mix
    o_ref[...] = _layer_norm(z, g_ref[...], b_ref[...])


def _proj_ln(a, w_bf16, h, g, b, alpha):
    t, d = h.shape
    tm = min(TM_PROJ, t)
    return pl.pallas_call(
        functools.partial(_proj_ln_kernel, alpha=alpha),
        out_shape=jax.ShapeDtypeStruct((t, d), F32),
        grid=(t // tm,),
        in_specs=[
            pl.BlockSpec((tm, d), lambda i: (i, 0)),
            pl.BlockSpec((d, d), lambda i: (0, 0), pipeline_mode=pl.Buffered(1)),
            pl.BlockSpec((tm, d), lambda i: (i, 0)),
            pl.BlockSpec((1, d), lambda i: (0, 0)),
            pl.BlockSpec((1, d), lambda i: (0, 0)),
        ],
        out_specs=pl.BlockSpec((tm, d), lambda i: (i, 0)),
        compiler_params=_params("parallel"),
        name="proj_ln",
    )(a, w_bf16, h, g.reshape(1, d), b.reshape(1, d))


def _pool_ln_kernel(h_ref, hp_ref, hn_ref, w_ref, sc_ref, g_ref, b_ref, o_ref, ext_ref, z_ref, *, alpha, seq):
    i = pl.program_id(0)
    tm, d = h_ref.shape
    c = d // len(POOL_WINDOWS)
    ext_ref[0:POOL_HALO, :] = jnp.where(i > 0, hp_ref[...], 0.0)
    ext_ref[POOL_HALO:POOL_HALO + tm, :] = h_ref[...]
    ext_ref[POOL_HALO + tm:2 * POOL_HALO + tm, :] = jnp.where(i < pl.num_programs(0) - 1, hn_ref[...], 0.0)
    t = i * tm + lax.broadcasted_iota(I32, (tm, 1), 0)
    for g, w in enumerate(POOL_WINDOWS):
        half = w // 2
        cols = slice(g * c, (g + 1) * c)
        win = ext_ref[POOL_HALO - half:POOL_HALO - half + tm, cols]
        for j in range(1 - half, half):
            win = win + ext_ref[POOL_HALO + j:POOL_HALO + j + tm, cols]
        cnt = (jnp.minimum(t + half, seq) - jnp.maximum(t - half, 0)).astype(F32)
        x = h_ref[:, cols]
        diff = win / cnt - x
        y = jnp.dot(diff.astype(BF16), w_ref[g], preferred_element_type=F32) * sc_ref[:, cols]
        z_ref[:, cols] = alpha * x + y
    o_ref[...] = _layer_norm(z_ref[...], g_ref[...], b_ref[...])


def _pool_ln(h, w_bf16, scale, g, b, alpha):
    t, d = h.shape
    tm = min(TM_POOL, t)
    hb = tm // POOL_HALO
    n_halo = t // POOL_HALO
    c = d // len(POOL_WINDOWS)
    return pl.pallas_call(
        functools.partial(_pool_ln_kernel, alpha=alpha, seq=t),
        out_shape=jax.ShapeDtypeStruct((t, d), F32),
        grid=(t // tm,),
        in_specs=[
            pl.BlockSpec((tm, d), lambda i: (i, 0)),
            pl.BlockSpec((POOL_HALO, d), lambda i: (jnp.maximum(i * hb - 1, 0), 0)),
            pl.BlockSpec((POOL_HALO, d), lambda i: (jnp.minimum((i + 1) * hb, n_halo - 1), 0)),
            pl.BlockSpec((len(POOL_WINDOWS), c, c), lambda i: (0, 0, 0)),
            pl.BlockSpec((1, d), lambda i: (0, 0)),
            pl.BlockSpec((1, d), lambda i: (0, 0)),
            pl.BlockSpec((1, d), lambda i: (0, 0)),
        ],
        out_specs=pl.BlockSpec((tm, d), lambda i: (i, 0)),
        scratch_shapes=[pltpu.VMEM((tm + 2 * POOL_HALO, d), F32), pltpu.VMEM((tm, d), F32)],
        compiler_params=_params("parallel"),
        name="pool_ln",
    )(h, h, h, w_bf16, scale.reshape(1, d), g.reshape(1, d), b.reshape(1, d))


def _split_bf16(x):
    hi = x.astype(BF16)
    lo = (x - hi.astype(F32)).astype(BF16)
    return hi, lo


def _router_kernel(h_ref, wt_ref, b_ref, idx_ref, gate_ref, rank_ref, cnt_ref, carry_ref, tri_ref):
    tm = h_ref.shape[0]
    n_exp = wt_ref.shape[0]
    n_grp, per = N_EXPERT_GROUPS, EXPERTS_PER_GROUP

    @pl.when(pl.program_id(0) == 0)
    def _():
        carry_ref[...] = jnp.zeros(carry_ref.shape, F32)
        r = lax.broadcasted_iota(I32, (tm, tm), 0)
        c = lax.broadcasted_iota(I32, (tm, tm), 1)
        tri_ref[...] = jnp.where(r <= c, 1.0, 0.0).astype(BF16)

    h_hi, h_lo = _split_bf16(h_ref[...])
    w_hi, w_lo = _split_bf16(wt_ref[...])
    dn = (((1,), (1,)), ((), ()))
    logits = (lax.dot_general(w_hi, h_hi, dn, preferred_element_type=F32)
              + lax.dot_general(w_hi, h_lo, dn, preferred_element_type=F32)
              + lax.dot_general(w_lo, h_hi, dn, preferred_element_type=F32))
    scores = 1.0 / (1.0 + jnp.exp(-logits))
    biased = scores + b_ref[...]

    v = biased.reshape(n_grp, per, tm)
    piota = lax.broadcasted_iota(I32, (n_grp, per, tm), 1)
    m1 = jnp.max(v, axis=1, keepdims=True)
    i1 = jnp.min(jnp.where(v == m1, piota, per), axis=1, keepdims=True)
    v2 = jnp.where(piota == i1, -jnp.inf, v)
    m2 = jnp.max(v2, axis=1, keepdims=True)
    i2 = jnp.min(jnp.where(v2 == m2, piota, per), axis=1, keepdims=True)
    grp_score = m1 + m2
    giota = lax.broadcasted_iota(I32, (n_grp, 1, tm), 0)
    best = jnp.min(jnp.where(grp_score == jnp.max(grp_score, axis=0, keepdims=True), giota, n_grp),
                   axis=0, keepdims=True)
    sel = giota == best
    e0 = jnp.sum(jnp.where(sel, giota * per + i1, 0), axis=0)
    e1 = jnp.sum(jnp.where(sel, giota * per + i2, 0), axis=0)

    eiota = lax.broadcasted_iota(I32, (n_exp, tm), 0)
    oh0 = eiota == e0
    oh1 = eiota == e1
    s0 = jnp.sum(jnp.where(oh0, scores, 0.0), axis=0, keepdims=True)
    s1 = jnp.sum(jnp.where(oh1, scores, 0.0), axis=0, keepdims=True)
    den = s0 + s1

    member = jnp.where(oh0, 1.0, jnp.where(oh1, 1.0, 0.0)).astype(BF16)
    cnt = jnp.dot(member, tri_ref[...], preferred_element_type=F32)
    tot = cnt + carry_ref[:, 0:1]
    r0 = jnp.sum(jnp.where(oh0, tot, 0.0), axis=0, keepdims=True) - 1.0
    r1 = jnp.sum(jnp.where(oh1, tot, 0.0), axis=0, keepdims=True) - 1.0
    carry_ref[...] = carry_ref[...] + cnt[:, tm - 1:tm]

    idx_ref[0:1, :] = e0
    idx_ref[1:2, :] = e1
    gate_ref[0:1, :] = s0 / den
    gate_ref[1:2, :] = s1 / den
    rank_ref[0:1, :] = r0.astype(I32)
    rank_ref[1:2, :] = r1.astype(I32)
    cnt_ref[...] = carry_ref[...]


def _router(h, router_wt, router_b):
    t, d = h.shape
    n_exp = router_wt.shape[0]
    tm = min(TM_ROUTER, t)
    pair = lambda dt: jax.ShapeDtypeStruct((TOP_K, t), dt)
    pair_spec = pl.BlockSpec((TOP_K, tm), lambda i: (0, i))
    return pl.pallas_call(
        _router_kernel,
        out_shape=(pair(I32), pair(F32), pair(I32), jax.ShapeDtypeStruct((n_exp, HEAD_DIM), F32)),
        grid=(t // tm,),
        in_specs=[
            pl.BlockSpec((tm, d), lambda i: (i, 0)),
            pl.BlockSpec((n_exp, d), lambda i: (0, 0)),
            pl.BlockSpec((n_exp, 1), lambda i: (0, 0)),
        ],
        out_specs=(pair_spec, pair_spec, pair_spec, pl.BlockSpec((n_exp, HEAD_DIM), lambda i: (0, 0))),
        scratch_shapes=[pltpu.VMEM((n_exp, HEAD_DIM), F32), pltpu.VMEM((tm, tm), BF16)],
        compiler_params=_params("arbitrary"),
        name="router",
    )(h, router_wt, router_b.reshape(n_exp, 1))


def _pos_kernel(idx_ref, rank_ref, starts_ref, pos_ref):
    n_exp = starts_ref.shape[0]
    tm = idx_ref.shape[1]
    eiota = lax.broadcasted_iota(I32, (n_exp, tm), 0)
    st = starts_ref[...]
    for k in range(TOP_K):
        oh = eiota == idx_ref[k:k + 1, :]
        pos_ref[k:k + 1, :] = jnp.sum(jnp.where(oh, st, 0), axis=0, keepdims=True) + rank_ref[k:k + 1, :]


def _positions(idx, rank, starts):
    t = idx.shape[1]
    n_exp = starts.shape[0]
    tm = min(TM_POS, t)
    pair_spec = pl.BlockSpec((TOP_K, tm), lambda i: (0, i))
    return pl.pallas_call(
        _pos_kernel,
        out_shape=jax.ShapeDtypeStruct((TOP_K, t), I32),
        grid=(t // tm,),
        in_specs=[pair_spec, pair_spec, pl.BlockSpec((n_exp, 1), lambda i: (0, 0))],
        out_specs=pair_spec,
        compiler_params=_params("parallel"),
        name="positions",
    )(idx, rank, starts.reshape(n_exp, 1))


def _invert_kernel(pos_ref, inv_ref):
    def body(i, carry):
        inv_ref[pos_ref[i]] = i
        return carry

    lax.fori_loop(0, pos_ref.shape[0], body, 0, unroll=8)


def _invert(pos_flat):
    return pl.pallas_call(
        _invert_kernel,
        out_shape=jax.ShapeDtypeStruct(pos_flat.shape, I32),
        grid_spec=pltpu.PrefetchScalarGridSpec(
            num_scalar_prefetch=1,
            grid=(1,),
            in_specs=[],
            out_specs=pl.BlockSpec(memory_space=pltpu.SMEM),
        ),
        compiler_params=_params("arbitrary"),
        name="invert",
    )(pos_flat)


def _expert_kernel(tile_ref, exp_ref, lo_ref, hi_ref, nw_ref, inv_ref, h_ref, win_ref, wout_ref, y_ref,
                   xbuf_ref, ybuf_ref, winb_ref, woutb_ref, gsem, ssem):
    w = pl.program_id(0)
    n_work = nw_ref[0]
    _, tm, d = xbuf_ref.shape
    n_tok = h_ref.shape[0]
    d_ff = wout_ref.shape[0]
    prev = jnp.maximum(w - 1, 0)
    new_expert = jnp.logical_or(w == 0, exp_ref[w] != exp_ref[prev])
    active = w < n_work

    def token_of(pair):
        if n_tok & (n_tok - 1) == 0:
            return pair & (n_tok - 1)
        return pair - (pair >= n_tok).astype(I32) * n_tok

    def gather_copy(item, row0, r):
        buf = item % 2
        t = token_of(inv_ref[row0 + r])
        return pltpu.make_async_copy(h_ref.at[pl.ds(t, 1), :], xbuf_ref.at[buf, pl.ds(r, 1), :], gsem.at[buf])

    def scatter_copy(item, row0, r):
        buf = item % 2
        return pltpu.make_async_copy(ybuf_ref.at[buf, pl.ds(r, 1), :], y_ref.at[pl.ds(inv_ref[row0 + r], 1), :],
                                     ssem.at[buf])

    def for_rows(item, fn):
        lo, hi = lo_ref[item], hi_ref[item]
        row0 = tile_ref[item] * tm
        n_groups = (hi - lo) // ROW_UNROLL

        def group(i, carry):
            for u in range(ROW_UNROLL):
                fn(item, row0, lo + i * ROW_UNROLL + u)
            return carry

        def single(r, carry):
            fn(item, row0, r)
            return carry

        lax.fori_loop(0, n_groups, group, 0)
        lax.fori_loop(lo + n_groups * ROW_UNROLL, hi, single, 0)

    start_gather = lambda item, row0, r: gather_copy(item, row0, r).start()
    wait_gather = lambda item, row0, r: gather_copy(item, row0, r).wait()
    start_scatter = lambda item, row0, r: scatter_copy(item, row0, r).start()
    wait_scatter = lambda item, row0, r: scatter_copy(item, row0, r).wait()

    @pl.when(w == 0)
    def _():
        xbuf_ref[...] = jnp.zeros(xbuf_ref.shape, F32)
        for_rows(0, start_gather)

    @pl.when(w + 1 < n_work)
    def _():
        for_rows(w + 1, start_gather)

    @pl.when(jnp.logical_and(active, new_expert))
    def _():
        winb_ref[...] = win_ref[...].astype(BF16)
        woutb_ref[...] = wout_ref[...].astype(BF16)

    @pl.when(active)
    def _():
        buf = w % 2
        for_rows(w, wait_gather)
        hid = jnp.dot(xbuf_ref[buf].astype(BF16), winb_ref[...], preferred_element_type=F32)
        hg, hu = hid[:, :d_ff], hid[:, d_ff:]
        act = hg * (1.0 / (1.0 + jnp.exp(-hg))) * hu
        ybuf_ref[buf] = jnp.dot(act.astype(BF16), woutb_ref[...], preferred_element_type=F32)
        for_rows(w, start_scatter)

        @pl.when(w >= 1)
        def _():
            for_rows(w - 1, wait_scatter)

        @pl.when(w == n_work - 1)
        def _():
            for_rows(w, wait_scatter)


def _work_items(counts, tm, n_rows):
    n_exp = counts.shape[0]
    n_tiles = n_rows // tm
    n_work = n_tiles + n_exp - 1
    ends = jnp.cumsum(counts)
    starts = ends - counts
    first = starts // tm
    last = jnp.maximum(ends - 1, starts) // tm
    n_items = jnp.where(counts > 0, last - first + 1, 0)
    item_end = jnp.cumsum(n_items)
    item_start = item_end - n_items
    total = item_end[-1]
    w = jnp.arange(n_work, dtype=I32)
    wc = jnp.minimum(w, total - 1)
    e = jnp.searchsorted(item_end, wc, side="right").astype(I32)
    tile = first[e] + (wc - item_start[e])
    lo = jnp.maximum(starts[e], tile * tm) - tile * tm
    hi = jnp.minimum(ends[e], (tile + 1) * tm) - tile * tm
    return tile.astype(I32), e, lo.astype(I32), hi.astype(I32), total.reshape(1).astype(I32), starts.astype(I32)


def _experts(h, w_in, w_out, layer, items, inv):
    t, d = h.shape
    two_f = w_in.shape[3]
    d_ff = w_out.shape[2]
    tm = min(TM_EXPERT, TOP_K * t)
    tile, e, lo, hi, total = items
    n_work = tile.shape[0]
    weights = lambda w, tl, ex, lo, hi, n, inv: (layer, ex[w], 0, 0)
    return pl.pallas_call(
        _expert_kernel,
        out_shape=jax.ShapeDtypeStruct((TOP_K * t, d), F32),
        grid_spec=pltpu.PrefetchScalarGridSpec(
            num_scalar_prefetch=6,
            grid=(n_work,),
            in_specs=[
                pl.BlockSpec(memory_space=pl.ANY),
                pl.BlockSpec((None, None, d, two_f), weights),
                pl.BlockSpec((None, None, d_ff, d), weights),
            ],
            out_specs=pl.BlockSpec(memory_space=pl.ANY),
            scratch_shapes=[
                pltpu.VMEM((2, tm, d), F32), pltpu.VMEM((2, tm, d), F32),
                pltpu.VMEM((d, two_f), BF16), pltpu.VMEM((d_ff, d), BF16),
                pltpu.SemaphoreType.DMA((2,)), pltpu.SemaphoreType.DMA((2,)),
            ],
        ),
        compiler_params=_params("arbitrary"),
        name="experts",
    )(tile, e, lo, hi, total, inv, h, w_in, w_out)


def _combine_kernel(h_ref, y0_ref, y1_ref, gate_ref, g_ref, b_ref, o_ref, *, alpha):
    gate = gate_ref[...]
    ffn = y0_ref[...] * gate[:, 0:1] + y1_ref[...] * gate[:, 1:2]
    z = alpha * h_ref[...] + ffn
    o_ref[...] = _layer_norm(z, g_ref[...], b_ref[...])


def _combine_ln(h, y, gate_tk, g, b, alpha):
    t, d = h.shape
    tm = min(TM_COMBINE, t)
    return pl.pallas_call(
        functools.partial(_combine_kernel, alpha=alpha),
        out_shape=jax.ShapeDtypeStruct((t, d), F32),
        grid=(t // tm,),
        in_specs=[
            pl.BlockSpec((tm, d), lambda i: (i, 0)),
            pl.BlockSpec((tm, d), lambda i: (i, 0)),
            pl.BlockSpec((tm, d), lambda i: (t // tm + i, 0)),
            pl.BlockSpec((tm, TOP_K), lambda i: (i, 0)),
            pl.BlockSpec((1, d), lambda i: (0, 0)),
            pl.BlockSpec((1, d), lambda i: (0, 0)),
        ],
        out_specs=pl.BlockSpec((tm, d), lambda i: (i, 0)),
        compiler_params=_params("parallel"),
        name="combine_ln",
    )(h, y, y, gate_tk, g.reshape(1, d), b.reshape(1, d))


def _moe_ln(h, router_wt, router_b, w_in, w_out, layer, g, b, alpha):
    t = h.shape[0]
    idx, gate, rank, cnt = _router(h, router_wt, router_b)
    counts = cnt[:, 0].astype(I32)
    tile, e, lo, hi, total, starts = _work_items(counts, min(TM_EXPERT, TOP_K * t), TOP_K * t)
    pos_flat = _positions(idx, rank, starts).reshape(TOP_K * t)
    y = _experts(h, w_in, w_out, layer, (tile, e, lo, hi, total), _invert(pos_flat))
    return _combine_ln(h, y, gate.T, g, b, alpha)


def kernel(x, w_qkv, lambda_qk, subln_g, w_o, pool_w, pool_scale, router_w, router_b, moe_w_in, moe_w_out,
           ln_g, ln_b):
    batch, seq, d = x.shape
    depth = ln_g.shape[0]
    alpha = (2.0 * depth) ** 0.25
    tables, cos_t, sin_t = _rope_tables(seq)
    router_wt = router_w.T
    outs = []
    for bi in range(batch):
        h = x[bi]
        for i in range(depth):
            j = i // N_MIXERS
            if i % N_MIXERS == 0:
                lam_init = 0.8 - 0.6 * math.exp(-0.3 * i)
                w_qv = jnp.concatenate([w_qkv[j, :, :d], w_qkv[j, :, 2 * d:]], axis=1).astype(BF16)
                qv = _qkv_proj(h, w_qv, tables)
                kt = _k_proj_t(h, w_qkv[j, :, d:2 * d].T.astype(BF16), cos_t, sin_t)
                att = _diff_attention(qv, kt, lambda_qk[j], subln_g[j], lam_init)
                h = _proj_ln(att, w_o[j].astype(BF16), h, ln_g[i, 0], ln_b[i, 0], alpha)
            else:
                h = _pool_ln(h, pool_w[j].astype(BF16), pool_scale[j], ln_g[i, 0], ln_b[i, 0], alpha)
            h = _moe_ln(h, router_wt, router_b, moe_w_in, moe_w_out, i, ln_g[i, 1], ln_b[i, 1], alpha)
        outs.append(h)
    return jnp.stack(outs)
```

```python
import functools
import math

import jax
import jax.numpy as jnp
from jax import lax
from jax.experimental import pallas as pl
from jax.experimental.pallas import tpu as pltpu

F32 = jnp.float32
BF16 = jnp.bfloat16
I32 = jnp.int32

HEAD_DIM = 128
V_HEAD_DIM = 2 * HEAD_DIM
ROT_DIM = HEAD_DIM // 4
ROT_HALF = ROT_DIM // 2
ROPE_THETA = 500000.0
POOL_WINDOWS = (2, 4, 8, 16)
POOL_HALO = 8
N_EXPERT_GROUPS = 8
EXPERTS_PER_GROUP = 8
TOP_K = 2
LN_EPS = 1e-5
RMS_EPS = 1e-5
N_MIXERS = 2
VMEM_LIMIT_BYTES = 56 * 1024 * 1024

TM_QKV, TN_QKV = 1024, 512
TQ_ATTN, TK_ATTN = 1024, 1024
Q_TILES_PER_STEP = 4
SHIFT_MARGIN = 1.001
SAFE_MIN_SUM = 2.0 ** -80
TM_PROJ = 512
TM_POOL = 512
TM_ROUTER = 512
TM_POS = 2048
TM_EXPERT = 256
TM_COMBINE = 512
ROW_UNROLL = 8


def _params(*sem):
    return pltpu.CompilerParams(dimension_semantics=sem, vmem_limit_bytes=VMEM_LIMIT_BYTES)


def _layer_norm(z, g, b):
    mu = jnp.mean(z, axis=-1, keepdims=True)
    zc = z - mu
    var = jnp.mean(zc * zc, axis=-1, keepdims=True)
    return zc * lax.rsqrt(var + LN_EPS) * g + b


def _rope_tables(seq):
    pos = jnp.arange(seq, dtype=F32)
    inv_freq = ROPE_THETA ** (-jnp.arange(0, ROT_DIM, 2, dtype=F32) / ROT_DIM)
    ang = pos[:, None] * inv_freq[None, :]
    cos, sin = jnp.cos(ang), jnp.sin(ang)
    rest = HEAD_DIM - ROT_DIM
    z_half = jnp.zeros((seq, ROT_HALF), F32)
    z_rest = jnp.zeros((seq, rest), F32)
    a = jnp.concatenate([cos, cos, jnp.ones((seq, rest), F32)], axis=1)
    b = jnp.concatenate([z_half, sin, z_rest], axis=1)
    c = jnp.concatenate([-sin, z_half, z_rest], axis=1)
    rot = jnp.stack([a, b, c])
    ident = jnp.stack([jnp.ones((seq, HEAD_DIM), F32), jnp.zeros((seq, HEAD_DIM), F32),
                       jnp.zeros((seq, HEAD_DIM), F32)])
    return jnp.stack([rot * (HEAD_DIM ** -0.5 * math.log2(math.e)), ident]), cos.T, sin.T


def _qkv_kernel(x_ref, w_ref, tab_ref, o_ref, xb_ref):
    @pl.when(pl.program_id(1) == 0)
    def _():
        xb_ref[...] = x_ref[...].astype(BF16)

    a, b, c = tab_ref[0], tab_ref[1], tab_ref[2]
    for j in range(o_ref.shape[1] // HEAD_DIM):
        if j % 2 == 0:
            wide = slice(j * HEAD_DIM, (j + 2) * HEAD_DIM)
            y = jnp.dot(xb_ref[...], w_ref[:, wide], preferred_element_type=F32)
        yj = y[:, (j % 2) * HEAD_DIM:(j % 2 + 1) * HEAD_DIM]
        out = yj * a + pltpu.roll(yj, ROT_HALF, 1) * b + pltpu.roll(yj, HEAD_DIM - ROT_HALF, 1) * c
        o_ref[:, j * HEAD_DIM:(j + 1) * HEAD_DIM] = out.astype(o_ref.dtype)


def _qkv_proj(h, w_bf16, tables):
    t, d = h.shape
    n = w_bf16.shape[1]
    tm, tn = min(TM_QKV, t), min(TN_QKV, d)
    return pl.pallas_call(
        _qkv_kernel,
        out_shape=jax.ShapeDtypeStruct((t, n), BF16),
        grid=(t // tm, n // tn),
        in_specs=[
            pl.BlockSpec((tm, d), lambda m, j: (m, 0)),
            pl.BlockSpec((d, tn), lambda m, j: (0, j)),
            pl.BlockSpec((None, 3, tm, HEAD_DIM), lambda m, j: ((j * tn) // d, 0, m, 0)),
        ],
        out_specs=pl.BlockSpec((tm, tn), lambda m, j: (m, j)),
        scratch_shapes=[pltpu.VMEM((tm, d), BF16)],
        compiler_params=_params("parallel", "arbitrary"),
        name="qkv_rope",
    )(h, w_bf16, tables)


def _kt_kernel(w_ref, x_ref, cos_ref, sin_ref, o_ref, xb_ref):
    @pl.when(pl.program_id(1) == 0)
    def _():
        xb_ref[...] = x_ref[...].astype(BF16)

    y = lax.dot_general(w_ref[...], xb_ref[...], (((1,), (1,)), ((), ())), preferred_element_type=F32)
    cos, sin = cos_ref[...], sin_ref[...]
    for g in range(y.shape[0] // HEAD_DIM):
        r0 = g * HEAD_DIM
        t1 = y[r0:r0 + ROT_HALF]
        t2 = y[r0 + ROT_HALF:r0 + ROT_DIM]
        o_ref[r0:r0 + ROT_HALF, :] = (t1 * cos - t2 * sin).astype(o_ref.dtype)
        o_ref[r0 + ROT_HALF:r0 + ROT_DIM, :] = (t2 * cos + t1 * sin).astype(o_ref.dtype)
        o_ref[r0 + ROT_DIM:r0 + HEAD_DIM, :] = y[r0 + ROT_DIM:r0 + HEAD_DIM].astype(o_ref.dtype)


def _k_proj_t(h, wt_bf16, cos_t, sin_t):
    t, d = h.shape
    n = wt_bf16.shape[0]
    tm, tn = min(TM_QKV, t), min(TN_QKV, n)
    return pl.pallas_call(
        _kt_kernel,
        out_shape=jax.ShapeDtypeStruct((n, t), BF16),
        grid=(t // tm, n // tn),
        in_specs=[
            pl.BlockSpec((tn, d), lambda m, j: (j, 0)),
            pl.BlockSpec((tm, d), lambda m, j: (m, 0)),
            pl.BlockSpec((ROT_HALF, tm), lambda m, j: (0, m)),
            pl.BlockSpec((ROT_HALF, tm), lambda m, j: (0, m)),
        ],
        out_specs=pl.BlockSpec((tn, tm), lambda m, j: (j, m)),
        scratch_shapes=[pltpu.VMEM((tm, d), BF16)],
        compiler_params=_params("parallel", "arbitrary"),
        name="k_rope_t",
    )(wt_bf16, h, cos_t, sin_t)


def _bound_attn_kernel(lam_ref, g_ref, q_ref, kt_ref, v_ref, o_ref,
                       p0_ref, p1_ref, kmax_ref, m_ref, ls_ref, acc_ref, *, tq, tk, lam_init):
    n = kt_ref.shape[1] // tk
    n_tiles = q_ref.shape[0] // tq
    p_refs = (p0_ref, p1_ref)
    comp = lambda c: slice(c * HEAD_DIM, (c + 1) * HEAD_DIM)

    def scores(u, c, start):
        return jnp.dot(q_ref[rows(u), comp(c)], kt_ref[comp(c), pl.ds(start, tk)], preferred_element_type=F32)

    lq = lam_ref[...]
    lam = (jnp.exp(jnp.sum(lq[0:1] * lq[1:2], axis=1, keepdims=True))
           - jnp.exp(jnp.sum(lq[2:3] * lq[3:4], axis=1, keepdims=True)) + lam_init)

    @pl.when(pl.program_id(1) == 0)
    def _():
        def chunk(i, carry):
            kk = kt_ref[:, pl.ds(pl.multiple_of(i * tk, tk), tk)].astype(F32)
            sq = kk * kk
            return tuple(jnp.maximum(carry[c], jnp.max(jnp.sum(sq[comp(c), :], axis=0, keepdims=True),
                                                       axis=1, keepdims=True)) for c in range(2))

        mx = lax.fori_loop(0, n, chunk, (jnp.zeros((1, 1), F32), jnp.zeros((1, 1), F32)))
        for c in range(2):
            kmax_ref[c] = jnp.broadcast_to(jnp.sqrt(mx[c]), kmax_ref.shape[1:])

    def rows(u):
        return pl.ds(pl.multiple_of(u * tq, tq), tq)

    def setup(st, u):
        for c in range(2):
            qc = q_ref[rows(u), comp(c)].astype(F32)
            qn = jnp.sqrt(jnp.sum(qc * qc, axis=1, keepdims=True))
            bound = qn * kmax_ref[c][0:1, 0:1] * SHIFT_MARGIN
            m_ref[st, c] = jnp.broadcast_to(bound, (tq, HEAD_DIM))
        ls_ref[st] = jnp.zeros(ls_ref.shape[1:], F32)
        acc_ref[st] = jnp.zeros(acc_ref.shape[1:], F32)

    def expo(st, u, j, slot):
        start = pl.multiple_of(j * tk, tk)
        for c in range(2):
            s = scores(u, c, start)
            shift = m_ref[st, c]
            ls = ls_ref[st, c]
            for b in range(tk // HEAD_DIM):
                lanes = slice(b * HEAD_DIM, (b + 1) * HEAD_DIM)
                p = jnp.exp2(s[:, lanes] - shift)
                ls = ls + p
                p_refs[slot][c, :, lanes] = p.astype(BF16)
            ls_ref[st, c] = ls

    def accumulate(st, j, slot):
        start = pl.multiple_of(j * tk, tk)
        vs = v_ref[pl.ds(start, tk), :]
        for c in range(2):
            acc_ref[st, c] = acc_ref[st, c] + jnp.dot(p_refs[slot][c], vs, preferred_element_type=F32)

    def write_out(o, u):
        ms = jnp.mean(o * o, axis=1, keepdims=True)
        of = o * lax.rsqrt(ms + RMS_EPS)
        of = of * g_ref[...] * (1.0 - lam_init)
        o_ref[rows(u), :] = of.astype(o_ref.dtype)

    def safe_tile(u):
        outs = []
        for c in range(2):
            def body(j, carry):
                m_prev, l_prev, acc = carry
                start = pl.multiple_of(j * tk, tk)
                s = scores(u, c, start)
                m_new = jnp.maximum(m_prev, jnp.max(s, axis=1, keepdims=True))
                alpha = jnp.exp2(m_prev - m_new)
                p = jnp.exp2(s - m_new)
                l_new = alpha * l_prev + jnp.sum(p, axis=1, keepdims=True)
                acc = alpha * acc + jnp.dot(p.astype(BF16), v_ref[pl.ds(start, tk), :],
                                            preferred_element_type=F32)
                return m_new, l_new, acc

            init = (jnp.full((tq, 1), -jnp.inf, F32), jnp.zeros((tq, 1), F32),
                    jnp.zeros((tq, V_HEAD_DIM), F32))
            _, l_fin, acc = lax.fori_loop(0, n, body, init)
            outs.append(acc / l_fin)
        write_out(outs[0] - lam * outs[1], u)

    def finalize(st, u):
        l0 = jnp.sum(ls_ref[st, 0], axis=1, keepdims=True)
        l1 = jnp.sum(ls_ref[st, 1], axis=1, keepdims=True)
        healthy = jnp.min(jnp.minimum(l0, l1)) >= SAFE_MIN_SUM

        @pl.when(healthy)
        def _():
            write_out(acc_ref[st, 0] / l0 - lam * (acc_ref[st, 1] / l1), u)

        @pl.when(jnp.logical_not(healthy))
        def _():
            safe_tile(u)

    if n == 1:
        def single(u, carry):
            setup(0, u)
            expo(0, u, 0, 0)
            accumulate(0, 0, 0)
            finalize(0, u)
            return carry

        lax.fori_loop(0, n_tiles, single, 0)
        return

    setup(0, 0)
    expo(0, 0, 0, 0)

    def tile(u, carry):
        st = u % 2
        nxt = jnp.minimum(u + 1, n_tiles - 1)

        def pair(i, c2):
            j = 2 * i
            expo(st, u, j + 1, 1)
            accumulate(st, j, 0)
            expo(st, u, j + 2, 0)
            accumulate(st, j + 1, 1)
            return c2

        lax.fori_loop(0, n // 2 - 1, pair, 0)
        expo(st, u, n - 1, 1)
        accumulate(st, n - 2, 0)
        setup(1 - st, nxt)
        expo(1 - st, nxt, 0, 0)
        accumulate(st, n - 1, 1)
        finalize(st, u)
        return carry

    lax.fori_loop(0, n_tiles, tile, 0)


def _diff_attention(qv, kt, lam_qk, subln_g, lam_init):
    t = qv.shape[0]
    d = qv.shape[1] // 2
    n_heads = d // V_HEAD_DIM
    tq, tk = min(TQ_ATTN, t), min(TK_ATTN, t)
    tq_step = min(tq * Q_TILES_PER_STEP, t)
    assert t // tk == 1 or (t // tk) % 2 == 0
    kern = functools.partial(_bound_attn_kernel, tq=tq, tk=tk, lam_init=lam_init)
    state = pltpu.VMEM((2, 2, tq, HEAD_DIM), F32)
    return pl.pallas_call(
        kern,
        out_shape=jax.ShapeDtypeStruct((t, d), BF16),
        grid=(n_heads, t // tq_step),
        in_specs=[
            pl.BlockSpec((4, HEAD_DIM), lambda h, i: (0, 0)),
            pl.BlockSpec((1, V_HEAD_DIM), lambda h, i: (0, 0)),
            pl.BlockSpec((tq_step, V_HEAD_DIM), lambda h, i: (i, h)),
            pl.BlockSpec((V_HEAD_DIM, t), lambda h, i: (h, 0), pipeline_mode=pl.Buffered(1)),
            pl.BlockSpec((t, V_HEAD_DIM), lambda h, i: (0, n_heads + h), pipeline_mode=pl.Buffered(1)),
        ],
        out_specs=pl.BlockSpec((tq_step, V_HEAD_DIM), lambda h, i: (i, h)),
        scratch_shapes=[
            pltpu.VMEM((2, tq, tk), BF16), pltpu.VMEM((2, tq, tk), BF16),
            pltpu.VMEM((2, 8, HEAD_DIM), F32),
            state, state,
            pltpu.VMEM((2, 2, tq, V_HEAD_DIM), F32),
        ],
        compiler_params=_params("arbitrary", "arbitrary"),
        name="diff_attn",
    )(lam_qk, subln_g.reshape(1, V_HEAD_DIM), qv, kt, qv)


def _proj_ln_kernel(a_ref, w_ref, h_ref, g_ref, b_ref, o_ref, *, alpha):
    mix = jnp.dot(a_ref[...], w_ref[...], preferred_element_type=F32)
    z = alpha * h_ref[...] + mix
    o_ref[...] = _layer_norm(z, g_ref[...], b_ref[...])


def _proj_ln(a, w_bf16, h, g, b, alpha):
    t, d = h.shape
    tm = min(TM_PROJ, t)
    return pl.pallas_call(
        functools.partial(_proj_ln_kernel, alpha=alpha),
        out_shape=jax.ShapeDtypeStruct((t, d), F32),
        grid=(t // tm,),
        in_specs=[
            pl.BlockSpec((tm, d), lambda i: (i, 0)),
            pl.BlockSpec((d, d), lambda i: (0, 0), pipeline_mode=pl.Buffered(1)),
            pl.BlockSpec((tm, d), lambda i: (i, 0)),
            pl.BlockSpec((1, d), lambda i: (0, 0)),
            pl.BlockSpec((1, d), lambda i: (0, 0)),
        ],
        out_specs=pl.BlockSpec((tm, d), lambda i: (i, 0)),
        compiler_params=_params("parallel"),
        name="proj_ln",
    )(a, w_bf16, h, g.reshape(1, d), b.reshape(1, d))


def _pool_ln_kernel(h_ref, hp_ref, hn_ref, w_ref, sc_ref, g_ref, b_ref, o_ref, ext_ref, z_ref, *, alpha, seq):
    i = pl.program_id(0)
    tm, d = h_ref.shape
    c = d // len(POOL_WINDOWS)
    ext_ref[0:POOL_HALO, :] = jnp.where(i > 0, hp_ref[...], 0.0)
    ext_ref[POOL_HALO:POOL_HALO + tm, :] = h_ref[...]
    ext_ref[POOL_HALO + tm:2 * POOL_HALO + tm, :] = jnp.where(i < pl.num_programs(0) - 1, hn_ref[...], 0.0)
    t = i * tm + lax.broadcasted_iota(I32, (tm, 1), 0)
    for g, w in enumerate(POOL_WINDOWS):
        half = w // 2
        cols = slice(g * c, (g + 1) * c)
        win = ext_ref[POOL_HALO - half:POOL_HALO - half + tm, cols]
        for j in range(1 - half, half):
            win = win + ext_ref[POOL_HALO + j:POOL_HALO + j + tm, cols]
        cnt = (jnp.minimum(t + half, seq) - jnp.maximum(t - half, 0)).astype(F32)
        x = h_ref[:, cols]
        diff = win / cnt - x
        y = jnp.dot(diff.astype(BF16), w_ref[g], preferred_element_type=F32) * sc_ref[:, cols]
        z_ref[:, cols] = alpha * x + y
    o_ref[...] = _layer_norm(z_ref[...], g_ref[...], b_ref[...])


def _pool_ln(h, w_bf16, scale, g, b, alpha):
    t, d = h.shape
    tm = min(TM_POOL, t)
    hb = tm // POOL_HALO
    n_halo = t // POOL_HALO
    c = d // len(POOL_WINDOWS)
    return pl.pallas_call(
        functools.partial(_pool_ln_kernel, alpha=alpha, seq=t),
        out_shape=jax.ShapeDtypeStruct((t, d), F32),
        grid=(t // tm,),
        in_specs=[
            pl.BlockSpec((tm, d), lambda i: (i, 0)),
            pl.BlockSpec((POOL_HALO, d), lambda i: (jnp.maximum(i * hb - 1, 0), 0)),
            pl.BlockSpec((POOL_HALO, d), lambda i: (jnp.minimum((i + 1) * hb, n_halo - 1), 0)),
            pl.BlockSpec((len(POOL_WINDOWS), c, c), lambda i: (0, 0, 0)),
            pl.BlockSpec((1, d), lambda i: (0, 0)),
            pl.BlockSpec((1, d), lambda i: (0, 0)),
            pl.BlockSpec((1, d), lambda i: (0, 0)),
        ],
        out_specs=pl.BlockSpec((tm, d), lambda i: (i, 0)),
        scratch_shapes=[pltpu.VMEM((tm + 2 * POOL_HALO, d), F32), pltpu.VMEM((tm, d), F32)],
        compiler_params=_params("parallel"),
        name="pool_ln",
    )(h, h, h, w_bf16, scale.reshape(1, d), g.reshape(1, d), b.reshape(1, d))


def _split_bf16(x):
    hi = x.astype(BF16)
    lo = (x - hi.astype(F32)).astype(BF16)
    return hi, lo


def _router_kernel(h_ref, wt_ref, b_ref, idx_ref, gate_ref, rank_ref, cnt_ref, carry_ref, tri_ref):
    tm = h_ref.shape[0]
    n_exp = wt_ref.shape[0]
    n_grp, per = N_EXPERT_GROUPS, EXPERTS_PER_GROUP

    @pl.when(pl.program_id(0) == 0)
    def _():
        carry_ref[...] = jnp.zeros(carry_ref.shape, F32)
        r = lax.broadcasted_iota(I32, (tm, tm), 0)
        c = lax.broadcasted_iota(I32, (tm, tm), 1)
        tri_ref[...] = jnp.where(r <= c, 1.0, 0.0).astype(BF16)

    h_hi, h_lo = _split_bf16(h_ref[...])
    w_hi, w_lo = _split_bf16(wt_ref[...])
    dn = (((1,), (1,)), ((), ()))
    logits = (lax.dot_general(w_hi, h_hi, dn, preferred_element_type=F32)
              + lax.dot_general(w_hi, h_lo, dn, preferred_element_type=F32)
              + lax.dot_general(w_lo, h_hi, dn, preferred_element_type=F32))
    scores = 1.0 / (1.0 + jnp.exp(-logits))
    biased = scores + b_ref[...]

    v = biased.reshape(n_grp, per, tm)
    piota = lax.broadcasted_iota(I32, (n_grp, per, tm), 1)
    m1 = jnp.max(v, axis=1, keepdims=True)
    i1 = jnp.min(jnp.where(v == m1, piota, per), axis=1, keepdims=True)
    v2 = jnp.where(piota == i1, -jnp.inf, v)
    m2 = jnp.max(v2, axis=1, keepdims=True)
    i2 = jnp.min(jnp.where(v2 == m2, piota, per), axis=1, keepdims=True)
    grp_score = m1 + m2
    giota = lax.broadcasted_iota(I32, (n_grp, 1, tm), 0)
    best = jnp.min(jnp.where(grp_score == jnp.max(grp_score, axis=0, keepdims=True), giota, n_grp),
                   axis=0, keepdims=True)
    sel = giota == best
    e0 = jnp.sum(jnp.where(sel, giota * per + i1, 0), axis=0)
    e1 = jnp.sum(jnp.where(sel, giota * per + i2, 0), axis=0)

    eiota = lax.broadcasted_iota(I32, (n_exp, tm), 0)
    oh0 = eiota == e0
    oh1 = eiota == e1
    s0 = jnp.sum(jnp.where(oh0, scores, 0.0), axis=0, keepdims=True)
    s1 = jnp.sum(jnp.where(oh1, scores, 0.0), axis=0, keepdims=True)
    den = s0 + s1

    member = jnp.where(oh0, 1.0, jnp.where(oh1, 1.0, 0.0)).astype(BF16)
    cnt = jnp.dot(member, tri_ref[...], preferred_element_type=F32)
    tot = cnt + carry_ref[:, 0:1]
    r0 = jnp.sum(jnp.where(oh0, tot, 0.0), axis=0, keepdims=True) - 1.0
    r1 = jnp.sum(jnp.where(oh1, tot, 0.0), axis=0, keepdims=True) - 1.0
    carry_ref[...] = carry_ref[...] + cnt[:, tm - 1:tm]

    idx_ref[0:1, :] = e0
    idx_ref[1:2, :] = e1
    gate_ref[0:1, :] = s0 / den
    gate_ref[1:2, :] = s1 / den
    rank_ref[0:1, :] = r0.astype(I32)
    rank_ref[1:2, :] = r1.astype(I32)
    cnt_ref[...] = carry_ref[...]


def _router(h, router_wt, router_b):
    t, d = h.shape
    n_exp = router_wt.shape[0]
    tm = min(TM_ROUTER, t)
    pair = lambda dt: jax.ShapeDtypeStruct((TOP_K, t), dt)
    pair_spec = pl.BlockSpec((TOP_K, tm), lambda i: (0, i))
    return pl.pallas_call(
        _router_kernel,
        out_shape=(pair(I32), pair(F32), pair(I32), jax.ShapeDtypeStruct((n_exp, HEAD_DIM), F32)),
        grid=(t // tm,),
        in_specs=[
            pl.BlockSpec((tm, d), lambda i: (i, 0)),
            pl.BlockSpec((n_exp, d), lambda i: (0, 0)),
            pl.BlockSpec((n_exp, 1), lambda i: (0, 0)),
        ],
        out_specs=(pair_spec, pair_spec, pair_spec, pl.BlockSpec((n_exp, HEAD_DIM), lambda i: (0, 0))),
        scratch_shapes=[pltpu.VMEM((n_exp, HEAD_DIM), F32), pltpu.VMEM((tm, tm), BF16)],
        compiler_params=_params("arbitrary"),
        name="router",
    )(h, router_wt, router_b.reshape(n_exp, 1))


def _pos_kernel(idx_ref, rank_ref, starts_ref, pos_ref):
    n_exp = starts_ref.shape[0]
    tm = idx_ref.shape[1]
    eiota = lax.broadcasted_iota(I32, (n_exp, tm), 0)
    st = starts_ref[...]
    for k in range(TOP_K):
        oh = eiota == idx_ref[k:k + 1, :]
        pos_ref[k:k + 1, :] = jnp.sum(jnp.where(oh, st, 0), axis=0, keepdims=True) + rank_ref[k:k + 1, :]


def _positions(idx, rank, starts):
    t = idx.shape[1]
    n_exp = starts.shape[0]
    tm = min(TM_POS, t)
    pair_spec = pl.BlockSpec((TOP_K, tm), lambda i: (0, i))
    return pl.pallas_call(
        _pos_kernel,
        out_shape=jax.ShapeDtypeStruct((TOP_K, t), I32),
        grid=(t // tm,),
        in_specs=[pair_spec, pair_spec, pl.BlockSpec((n_exp, 1), lambda i: (0, 0))],
        out_specs=pair_spec,
        compiler_params=_params("parallel"),
        name="positions",
    )(idx, rank, starts.reshape(n_exp, 1))


def _invert_kernel(pos_ref, inv_ref):
    def body(i, carry):
        inv_ref[pos_ref[i]] = i
        return carry

    lax.fori_loop(0, pos_ref.shape[0], body, 0, unroll=8)


def _invert(pos_flat):
    return pl.pallas_call(
        _invert_kernel,
        out_shape=jax.ShapeDtypeStruct(pos_flat.shape, I32),
        grid_spec=pltpu.PrefetchScalarGridSpec(
            num_scalar_prefetch=1,
            grid=(1,),
            in_specs=[],
            out_specs=pl.BlockSpec(memory_space=pltpu.SMEM),
        ),
        compiler_params=_params("arbitrary"),
        name="invert",
    )(pos_flat)


def _expert_kernel(tile_ref, exp_ref, lo_ref, hi_ref, nw_ref, inv_ref, h_ref, win_ref, wout_ref, y_ref,
                   xbuf_ref, ybuf_ref, winb_ref, woutb_ref, gsem, ssem):
    w = pl.program_id(0)
    n_work = nw_ref[0]
    _, tm, d = xbuf_ref.shape
    n_tok = h_ref.shape[0]
    d_ff = wout_ref.shape[0]
    prev = jnp.maximum(w - 1, 0)
    new_expert = jnp.logical_or(w == 0, exp_ref[w] != exp_ref[prev])
    active = w < n_work

    def token_of(pair):
        if n_tok & (n_tok - 1) == 0:
            return pair & (n_tok - 1)
        return pair - (pair >= n_tok).astype(I32) * n_tok

    def gather_copy(item, row0, r):
        buf = item % 2
        t = token_of(inv_ref[row0 + r])
        return pltpu.make_async_copy(h_ref.at[pl.ds(t, 1), :], xbuf_ref.at[buf, pl.ds(r, 1), :], gsem.at[buf])

    def scatter_copy(item, row0, r):
        buf = item % 2
        return pltpu.make_async_copy(ybuf_ref.at[buf, pl.ds(r, 1), :], y_ref.at[pl.ds(inv_ref[row0 + r], 1), :],
                                     ssem.at[buf])

    def for_rows(item, fn):
        lo, hi = lo_ref[item], hi_ref[item]
        row0 = tile_ref[item] * tm
        n_groups = (hi - lo) // ROW_UNROLL

        def group(i, carry):
            for u in range(ROW_UNROLL):
                fn(item, row0, lo + i * ROW_UNROLL + u)
            return carry

        def single(r, carry):
            fn(item, row0, r)
            return carry

        lax.fori_loop(0, n_groups, group, 0)
        lax.fori_loop(lo + n_groups * ROW_UNROLL, hi, single, 0)

    start_gather = lambda item, row0, r: gather_copy(item, row0, r).start()
    wait_gather = lambda item, row0, r: gather_copy(item, row0, r).wait()
    start_scatter = lambda item, row0, r: scatter_copy(item, row0, r).start()
    wait_scatter = lambda item, row0, r: scatter_copy(item, row0, r).wait()

    @pl.when(w == 0)
    def _():
        xbuf_ref[...] = jnp.zeros(xbuf_ref.shape, F32)
        for_rows(0, start_gather)

    @pl.when(w + 1 < n_work)
    def _():
        for_rows(w + 1, start_gather)

    @pl.when(jnp.logical_and(active, new_expert))
    def _():
        winb_ref[...] = win_ref[...].astype(BF16)
        woutb_ref[...] = wout_ref[...].astype(BF16)

    @pl.when(active)
    def _():
        buf = w % 2
        for_rows(w, wait_gather)
        hid = jnp.dot(xbuf_ref[buf].astype(BF16), winb_ref[...], preferred_element_type=F32)
        hg, hu = hid[:, :d_ff], hid[:, d_ff:]
        act = hg * (1.0 / (1.0 + jnp.exp(-hg))) * hu
        ybuf_ref[buf] = jnp.dot(act.astype(BF16), woutb_ref[...], preferred_element_type=F32)
        for_rows(w, start_scatter)

        @pl.when(w >= 1)
        def _():
            for_rows(w - 1, wait_scatter)

        @pl.when(w == n_work - 1)
        def _():
            for_rows(w, wait_scatter)


def _work_items(counts, tm, n_rows):
    n_exp = counts.shape[0]
    n_tiles = n_rows // tm
    n_work = n_tiles + n_exp - 1
    ends = jnp.cumsum(counts)
    starts = ends - counts
    first = starts // tm
    last = jnp.maximum(ends - 1, starts) // tm
    n_items = jnp.where(counts > 0, last - first + 1, 0)
    item_end = jnp.cumsum(n_items)
    item_start = item_end - n_items
    total = item_end[-1]
    w = jnp.arange(n_work, dtype=I32)
    wc = jnp.minimum(w, total - 1)
    e = jnp.searchsorted(item_end, wc, side="right").astype(I32)
    tile = first[e] + (wc - item_start[e])
    lo = jnp.maximum(starts[e], tile * tm) - tile * tm
    hi = jnp.minimum(ends[e], (tile + 1) * tm) - tile * tm
    return tile.astype(I32), e, lo.astype(I32), hi.astype(I32), total.reshape(1).astype(I32), starts.astype(I32)


def _experts(h, w_in, w_out, layer, items, inv):
    t, d = h.shape
    two_f = w_in.shape[3]
    d_ff = w_out.shape[2]
    tm = min(TM_EXPERT, TOP_K * t)
    tile, e, lo, hi, total = items
    n_work = tile.shape[0]
    weights = lambda w, tl, ex, lo, hi, n, inv: (layer, ex[w], 0, 0)
    return pl.pallas_call(
        _expert_kernel,
        out_shape=jax.ShapeDtypeStruct((TOP_K * t, d), F32),
        grid_spec=pltpu.PrefetchScalarGridSpec(
            num_scalar_prefetch=6,
            grid=(n_work,),
            in_specs=[
                pl.BlockSpec(memory_space=pl.ANY),
                pl.BlockSpec((None, None, d, two_f), weights),
                pl.BlockSpec((None, None, d_ff, d), weights),
            ],
            out_specs=pl.BlockSpec(memory_space=pl.ANY),
            scratch_shapes=[
                pltpu.VMEM((2, tm, d), F32), pltpu.VMEM((2, tm, d), F32),
                pltpu.VMEM((d, two_f), BF16), pltpu.VMEM((d_ff, d), BF16),
                pltpu.SemaphoreType.DMA((2,)), pltpu.SemaphoreType.DMA((2,)),
            ],
        ),
        compiler_params=_params("arbitrary"),
        name="experts",
    )(tile, e, lo, hi, total, inv, h, w_in, w_out)


def _combine_kernel(h_ref, y0_ref, y1_ref, gate_ref, g_ref, b_ref, o_ref, *, alpha):
    gate = gate_ref[...]
    ffn = y0_ref[...] * gate[:, 0:1] + y1_ref[...] * gate[:, 1:2]
    z = alpha * h_ref[...] + ffn
    o_ref[...] = _layer_norm(z, g_ref[...], b_ref[...])


def _combine_ln(h, y, gate_tk, g, b, alpha):
    t, d = h.shape
    tm = min(TM_COMBINE, t)
    return pl.pallas_call(
        functools.partial(_combine_kernel, alpha=alpha),
        out_shape=jax.ShapeDtypeStruct((t, d), F32),
        grid=(t // tm,),
        in_specs=[
            pl.BlockSpec((tm, d), lambda i: (i, 0)),
            pl.BlockSpec((tm, d), lambda i: (i, 0)),
            pl.BlockSpec((tm, d), lambda i: (t // tm + i, 0)),
            pl.BlockSpec((tm, TOP_K), lambda i: (i, 0)),
            pl.BlockSpec((1, d), lambda i: (0, 0)),
            pl.BlockSpec((1, d), lambda i: (0, 0)),
        ],
        out_specs=pl.BlockSpec((tm, d), lambda i: (i, 0)),
        compiler_params=_params("parallel"),
        name="combine_ln",
    )(h, y, y, gate_tk, g.reshape(1, d), b.reshape(1, d))


def _moe_ln(h, router_wt, router_b, w_in, w_out, layer, g, b, alpha):
    t = h.shape[0]
    idx, gate, rank, cnt = _router(h, router_wt, router_b)
    counts = cnt[:, 0].astype(I32)
    tile, e, lo, hi, total, starts = _work_items(counts, min(TM_EXPERT, TOP_K * t), TOP_K * t)
    pos_flat = _positions(idx, rank, starts).reshape(TOP_K * t)
    y = _experts(h, w_in, w_out, layer, (tile, e, lo, hi, total), _invert(pos_flat))
    return _combine_ln(h, y, gate.T, g, b, alpha)


def kernel(x, w_qkv, lambda_qk, subln_g, w_o, pool_w, pool_scale, router_w, router_b, moe_w_in, moe_w_out,
           ln_g, ln_b):
    batch, seq, d = x.shape
    depth = ln_g.shape[0]
    alpha = (2.0 * depth) ** 0.25
    tables, cos_t, sin_t = _rope_tables(seq)
    router_wt = router_w.T
    outs = []
    for bi in range(batch):
        h = x[bi]
        for i in range(depth):
            j = i // N_MIXERS
            if i % N_MIXERS == 0:
                lam_init = 0.8 - 0.6 * math.exp(-0.3 * i)
                w_qv = jnp.concatenate([w_qkv[j, :, :d], w_qkv[j, :, 2 * d:]], axis=1).astype(BF16)
                qv = _qkv_proj(h, w_qv, tables)
                kt = _k_proj_t(h, w_qkv[j, :, d:2 * d].T.astype(BF16), cos_t, sin_t)
                att = _diff_attention(qv, kt, lambda_qk[j], subln_g[j], lam_init)
                h = _proj_ln(att, w_o[j].astype(BF16), h, ln_g[i, 0], ln_b[i, 0], alpha)
            else:
                h = _pool_ln(h, pool_w[j].astype(BF16), pool_scale[j], ln_g[i, 0], ln_b[i, 0], alpha)
            h = _moe_ln(h, router_wt, router_b, moe_w_in, moe_w_out, i, ln_g[i, 1], ln_b[i, 1], alpha)
        outs.append(h)
    return jnp.stack(outs)
```
